```python
import jax, jax.numpy as jnp
from jax import lax
import numpy as np

D_MODEL = 1024
BATCH = 8
SEQ = 2048
DEPTH = 2

N_GMLP_GROUPS = 8
GMLP_GROUP_DIM = 64
GMLP_WIDTH = N_GMLP_GROUPS * GMLP_GROUP_DIM
CHUNK = 128
N_SB_HEADS = 8
SB_HEAD_DIM = 64
SB_WIDTH = N_SB_HEADS * SB_HEAD_DIM
SB_BLOCK = 128
EVEN_IN_WIDTH = 2 * GMLP_WIDTH + 3 * SB_WIDTH
EVEN_MIX_WIDTH = GMLP_WIDTH + SB_WIDTH
CONV_WIDTH = D_MODEL
CONV_K = 3
N_GROUPS = 4
EXPERTS_PER_GROUP = 8
N_EXPERTS = N_GROUPS * EXPERTS_PER_GROUP
EXPERT_TOP_K = 2
EXPERT_FF = 512
EPS = 1e-6
N_EVEN = (DEPTH + 1) // 2
N_ODD = DEPTH // 2

kernel_name = "hybrid_gmlp_stickbreak_shortconv_hmoe"


def rms_norm(x, gain):
    xf = x.astype(jnp.float32)
    y = xf * lax.rsqrt(jnp.mean(xf * xf, axis=-1, keepdims=True) + EPS)
    return (y * gain.astype(jnp.float32)).astype(x.dtype)


def chunked_spatial_gating(u, v, w_s, b_s, v_gain):
    b, s, g, d = v.shape
    v = rms_norm(v, v_gain)
    causal = jnp.tril(jnp.ones((CHUNK, CHUNK), dtype=bool))
    w = jnp.where(causal[None], w_s, jnp.zeros_like(w_s)).astype(v.dtype)
    vc = v.reshape(b, s // CHUNK, CHUNK, g, d)
    mixed = jnp.einsum('gts,bnsgd->bntgd', w, vc) + b_s.T.astype(v.dtype)[None, None, :, :, None]
    return u * mixed.reshape(b, s, g, d)


def stick_breaking_attention(q, k, v):
    b, s, h, d = q.shape
    nb = s // SB_BLOCK
    scale = d ** -0.5
    qb = q.reshape(b, nb, SB_BLOCK, h, d).transpose(1, 0, 2, 3, 4)
    kpos = jnp.arange(s)

    def block(args):
        qi, i = args
        z = jnp.einsum('bthd,bshd->bhts', qi, k).astype(jnp.float32) * scale
        qpos = i * SB_BLOCK + jnp.arange(SB_BLOCK)
        causal = kpos[None, :] < qpos[:, None]
        log_beta = jax.nn.log_sigmoid(z)
        log_keep = jnp.where(causal, jax.nn.log_sigmoid(-z), 0.0)
        suffix = lax.cumsum(log_keep, axis=3, reverse=True) - log_keep
        a = jnp.where(causal, jnp.exp(log_beta + suffix), 0.0)
        return jnp.einsum('bhts,bshd->bthd', a.astype(v.dtype), v)

    out = lax.map(block, (qb, jnp.arange(nb)))
    return out.transpose(1, 0, 2, 3, 4).reshape(b, s, h, d)


def short_conv_mixer(p, conv_w):
    gate_b, gate_c, h = jnp.split(p, 3, axis=-1)
    y = gate_c * h
    conv = lax.conv_general_dilated(
        y, conv_w.astype(y.dtype)[:, None, :],
        window_strides=(1,), padding=[(CONV_K - 1, 0)],
        dimension_numbers=('NWC', 'WIO', 'NWC'),
        feature_group_count=CONV_WIDTH)
    return gate_b * conv


def hierarchical_moe(x, w_router_group, w_router_expert, w_gate, w_up, w_down):
    b, s, d = x.shape
    t = x.reshape(-1, d)
    group_prob = jax.nn.softmax((t @ w_router_group).astype(jnp.float32), axis=-1)
    top_gp, top_g = lax.top_k(group_prob, 1)
    top_gp, top_g = top_gp[:, 0], top_g[:, 0]
    expert_logits = jnp.einsum('td,gde->tge', t, w_router_expert).astype(jnp.float32)
    sel_logits = jnp.take_along_axis(expert_logits, top_g[:, None, None], axis=1)[:, 0]
    top_el, top_e = lax.top_k(sel_logits, EXPERT_TOP_K)
    top_ep = jax.nn.softmax(top_el, axis=-1)
    expert_idx = top_g[:, None] * EXPERTS_PER_GROUP + top_e
    weights = top_gp[:, None] * top_ep
    combine = jnp.sum(jax.nn.one_hot(expert_idx, N_EXPERTS, dtype=jnp.float32) * weights[..., None], axis=1)
    combine = combine.astype(t.dtype)
    out = jnp.zeros_like(t)
    for e in range(N_EXPERTS):
        h = jax.nn.silu(t @ w_gate[e]) * (t @ w_up[e])
        out = out + combine[:, e:e + 1] * (h @ w_down[e])
    return out.reshape(b, s, d)


def setup_inputs(seed: int = 0) -> dict:
    key = jax.random.key(seed)
    ks = jax.random.split(key, 18)
    f32 = jnp.float32
    nrm = lambda k, shape, scale: jax.random.normal(k, shape, f32) * scale
    return {
        "x": jax.random.normal(ks[0], (BATCH, SEQ, D_MODEL), f32),
        "even_w_in": nrm(ks[1], (N_EVEN, D_MODEL, EVEN_IN_WIDTH), D_MODEL ** -0.5),
        "even_w_out": nrm(ks[2], (N_EVEN, EVEN_MIX_WIDTH, D_MODEL), EVEN_MIX_WIDTH ** -0.5),
        "gmlp_w_s": nrm(ks[3], (N_EVEN, N_GMLP_GROUPS, CHUNK, CHUNK), CHUNK ** -0.5),
        "gmlp_b_s": 1.0 + nrm(ks[4], (N_EVEN, N_GMLP_GROUPS, CHUNK), 0.1),
        "gmlp_v_gain": 1.0 + nrm(ks[5], (N_EVEN, N_GMLP_GROUPS, GMLP_GROUP_DIM), 0.05),
        "odd_w_in": nrm(ks[6], (N_ODD, D_MODEL, 3 * CONV_WIDTH), D_MODEL ** -0.5),
        "odd_conv_w": nrm(ks[7], (N_ODD, CONV_K, CONV_WIDTH), CONV_K ** -0.5),
        "odd_w_out": nrm(ks[8], (N_ODD, CONV_WIDTH, D_MODEL), CONV_WIDTH ** -0.5),
        "norm_mix": 1.0 + nrm(ks[9], (DEPTH, D_MODEL), 0.05),
        "norm_ffn": 1.0 + nrm(ks[10], (DEPTH, D_MODEL), 0.05),
        "router_group": nrm(ks[11], (DEPTH, D_MODEL, N_GROUPS), D_MODEL ** -0.5),
        "router_expert": nrm(ks[12], (DEPTH, N_GROUPS, D_MODEL, EXPERTS_PER_GROUP), D_MODEL ** -0.5),
        "w_gate": nrm(ks[13], (DEPTH, N_EXPERTS, D_MODEL, EXPERT_FF), D_MODEL ** -0.5),
        "w_up": nrm(ks[14], (DEPTH, N_EXPERTS, D_MODEL, EXPERT_FF), D_MODEL ** -0.5),
        "w_down": nrm(ks[15], (DEPTH, N_EXPERTS, EXPERT_FF, D_MODEL), EXPERT_FF ** -0.5),
        "norm_final": 1.0 + nrm(ks[16], (D_MODEL,), 0.05),
    }


def reference(x, even_w_in, even_w_out, gmlp_w_s, gmlp_b_s, gmlp_v_gain,
              odd_w_in, odd_conv_w, odd_w_out, norm_mix, norm_ffn,
              router_group, router_expert, w_gate, w_up, w_down, norm_final):
    b, s, _ = x.shape
    for l in range(DEPTH):
        h = rms_norm(x, norm_mix[l])
        i = l // 2
        if l % 2 == 0:
            p = h @ even_w_in[i]
            u, v, q, k, vv = jnp.split(
                p, [GMLP_WIDTH, 2 * GMLP_WIDTH, 2 * GMLP_WIDTH + SB_WIDTH,
                    2 * GMLP_WIDTH + 2 * SB_WIDTH], axis=-1)
            u = jax.nn.gelu(u, approximate=False).reshape(b, s, N_GMLP_GROUPS, GMLP_GROUP_DIM)
            v = jax.nn.gelu(v, approximate=False).reshape(b, s, N_GMLP_GROUPS, GMLP_GROUP_DIM)
            a_out = chunked_spatial_gating(u, v, gmlp_w_s[i], gmlp_b_s[i], gmlp_v_gain[i])
            heads = lambda t: t.reshape(b, s, N_SB_HEADS, SB_HEAD_DIM)
            b_out = stick_breaking_attention(heads(q), heads(k), heads(vv))
            mix = jnp.concatenate([a_out.reshape(b, s, GMLP_WIDTH),
                                   b_out.reshape(b, s, SB_WIDTH)], axis=-1) @ even_w_out[i]
        else:
            mix = short_conv_mixer(h @ odd_w_in[i], odd_conv_w[i]) @ odd_w_out[i]
        x = x + mix
        x = x + hierarchical_moe(rms_norm(x, norm_ffn[l]), router_group[l], router_expert[l],
                                 w_gate[l], w_up[l], w_down[l])
    return rms_norm(x, norm_final)
```

```python
import functools

import jax
import jax.numpy as jnp
from jax import lax
from jax.experimental import pallas as pl
from jax.experimental.pallas import tpu as pltpu

F32 = jnp.float32
BF16 = jnp.bfloat16

D_MODEL = 1024
N_PAIR = 4
HEAD_DIM = 64
MIX_W = 512
CHUNK = 128
N_GROUPS = 4
EXPERTS_PER_GROUP = 8
N_EXPERTS = 32
EXPERT_FF = 512
EPS = 1e-6
LANES = 128

ROUTER_LANES = 128
EXPERT_LANE0 = N_GROUPS

TM_PROJ = 512
TM_MOE = 256
TM_DISPATCH = 512
TM_COMBINE = 256
VMEM_LIMIT = 56 * 1024 * 1024


def _rms_norm(x, gain):
    return x * lax.rsqrt(jnp.mean(x * x, axis=-1, keepdims=True) + EPS) * gain


def _split_bf16(x):
    hi = x.astype(BF16)
    lo = (x - hi.astype(F32)).astype(BF16)
    return hi, lo


def _dot(a, b):
    return jnp.dot(a, b, preferred_element_type=F32)


def _even_inproj_kernel(x_ref, gain_ref, w_ref, uv_ref, q_ref, k_ref, v_ref):
    h = _rms_norm(x_ref[...], gain_ref[...]).astype(BF16)
    uv = _dot(h, w_ref[:, 0:2 * MIX_W])
    uv = 0.5 * uv * (1.0 + lax.erf(uv * (0.5 ** 0.5)))
    uv_ref[...] = uv.astype(BF16)
    q_ref[...] = (_dot(h, w_ref[:, 2 * MIX_W:3 * MIX_W]) * (HEAD_DIM ** -0.5)).astype(BF16)
    k_ref[...] = _dot(h, w_ref[:, 3 * MIX_W:4 * MIX_W]).astype(BF16)
    v_ref[...] = _dot(h, w_ref[:, 4 * MIX_W:5 * MIX_W]).astype(BF16)


def _even_inproj(x, gain, w_in):
    t = x.shape[0]
    row = lambda w: pl.BlockSpec((TM_PROJ, w), lambda i: (i, 0))
    return pl.pallas_call(
        _even_inproj_kernel,
        grid=(t // TM_PROJ,),
        in_specs=[row(D_MODEL),
                  pl.BlockSpec((1, D_MODEL), lambda i: (0, 0)),
                  pl.BlockSpec((D_MODEL, 5 * MIX_W), lambda i: (0, 0))],
        out_specs=[row(2 * MIX_W), row(MIX_W), row(MIX_W), row(MIX_W)],
        out_shape=[jax.ShapeDtypeStruct((t, 2 * MIX_W), BF16)] + [jax.ShapeDtypeStruct((t, MIX_W), BF16)] * 3,
        compiler_params=pltpu.CompilerParams(dimension_semantics=("arbitrary",), vmem_limit_bytes=VMEM_LIMIT),
        name="even_inproj",
    )(x, gain, w_in)


def _gate_attn_kernel(uv_ref, q_ref, k_ref, v_ref, ws_ref, bias_ref, vgain_ref, gmat_ref, out_ref):
    i = pl.program_id(1)
    row = lax.broadcasted_iota(jnp.int32, (CHUNK, CHUNK), 0)
    col = lax.broadcasted_iota(jnp.int32, (CHUNK, CHUNK), 1)
    first_half = col < HEAD_DIM

    u = uv_ref[:, 0:MIX_W].astype(F32)
    v = uv_ref[:, MIX_W:2 * MIX_W].astype(F32)
    sq_hi, sq_lo = _split_bf16(v * v)
    mean_sq = (_dot(sq_hi, gmat_ref[...]) + _dot(sq_lo, gmat_ref[...])) * (1.0 / HEAD_DIM)
    vn = (v * lax.rsqrt(mean_sq + EPS) * vgain_ref[...]).astype(BF16)
    tril = row >= col
    for p in range(N_PAIR):
        cols = slice(p * LANES, (p + 1) * LANES)
        vp = vn[:, cols]
        m0 = _dot(jnp.where(tril, ws_ref[2 * p], 0.0).astype(BF16), vp)
        m1 = _dot(jnp.where(tril, ws_ref[2 * p + 1], 0.0).astype(BF16), vp)
        mixed = jnp.where(first_half, m0, m1) + bias_ref[:, cols]
        out_ref[:, cols] = (u[:, cols] * mixed).astype(out_ref.dtype)

    strict = col < row
    suffix_mat = (row > col).astype(BF16)

    def block(qm, pcols, j, carry, acc, mask):
        rows = pl.ds(pl.multiple_of(j * CHUNK, CHUNK), CHUNK)
        z = lax.dot_general(qm, k_ref[rows, pcols], (((1,), (1,)), ((), ())), preferred_element_type=F32)
        softplus_tail = jnp.log(1.0 + jnp.exp(-jnp.abs(z)))
        log_beta = jnp.minimum(z, 0.0) - softplus_tail
        log_keep = log_beta - z
        if mask is not None:
            log_keep = jnp.where(mask, log_keep, 0.0)
        keep_hi, keep_lo = _split_bf16(log_keep)
        suffix = _dot(keep_hi, suffix_mat) + _dot(keep_lo, suffix_mat)
        a = jnp.exp(log_beta + suffix + carry)
        if mask is not None:
            a = jnp.where(mask, a, 0.0)
        acc = acc + _dot(a.astype(BF16), v_ref[rows, pcols])
        carry = carry + jnp.sum(log_keep, axis=1, keepdims=True)
        return carry, acc

    for p in range(N_PAIR):
        pcols = slice(p * LANES, (p + 1) * LANES)
        qp = q_ref[:, pcols]
        accs = []
        for half in range(2):
            qm = jnp.where(first_half if half == 0 else jnp.logical_not(first_half), qp, jnp.zeros_like(qp))
            carry = jnp.zeros((CHUNK, 1), F32)
            acc = jnp.zeros((CHUNK, LANES), F32)
            carry, acc = block(qm, pcols, i, carry, acc, strict)

            def body(step, state, qm=qm, pcols=pcols):
                return block(qm, pcols, i - 1 - step, state[0], state[1], None)

            carry, acc = lax.fori_loop(0, i, body, (carry, acc))
            accs.append(acc)
        out_ref[:, MIX_W + p * LANES:MIX_W + (p + 1) * LANES] = (
            jnp.where(first_half, accs[0], accs[1]).astype(out_ref.dtype))


def _gate_attn(uv, q, k, v, w_s, bias_full, v_gain, gmat, batch, seq):
    nq = seq // CHUNK
    t = uv.shape[0]
    qrow = lambda w: pl.BlockSpec((CHUNK, w), lambda b, i: (b * nq + i, 0))
    kv = pl.BlockSpec((seq, MIX_W), lambda b, i: (b, 0))
    full = lambda shape: pl.BlockSpec(shape, lambda b, i: (0,) * len(shape))
    return pl.pallas_call(
        _gate_attn_kernel,
        grid=(batch, nq),
        in_specs=[qrow(2 * MIX_W), qrow(MIX_W), kv, kv,
                  full((2 * N_PAIR, CHUNK, CHUNK)), full((CHUNK, MIX_W)), full((1, MIX_W)),
                  full((MIX_W, MIX_W))],
        out_specs=qrow(2 * MIX_W),
        out_shape=jax.ShapeDtypeStruct((t, 2 * MIX_W), BF16),
        compiler_params=pltpu.CompilerParams(dimension_semantics=("arbitrary", "arbitrary"),
                                             vmem_limit_bytes=VMEM_LIMIT),
        name="gate_attn",
    )(uv, q, k, v, w_s, bias_full, v_gain, gmat)


def _router_epilogue(x1, gain_ref, wr_hi_ref, wr_lo_ref, ltri_ref, base_ref, is_first,
                     xn_ref, rinfo_ref, counts_ref):
    tm = x1.shape[0]
    xn = _rms_norm(x1, gain_ref[...])
    xn_ref[...] = xn
    x_hi, x_lo = _split_bf16(xn)
    logits = _dot(x_hi, wr_hi_ref[...]) + _dot(x_lo, wr_hi_ref[...]) + _dot(x_hi, wr_lo_ref[...])

    lane = lax.broadcasted_iota(jnp.int32, (tm, ROUTER_LANES), 1)
    neg_inf = jnp.float32(-jnp.inf)
    big = jnp.int32(ROUTER_LANES)
    is_group = lane < N_GROUPS
    lg = jnp.where(is_group, logits, neg_inf)
    gmax = jnp.max(lg, axis=1, keepdims=True)
    gsum = jnp.sum(jnp.where(is_group, jnp.exp(logits - gmax), 0.0), axis=1, keepdims=True)
    p_top = 1.0 / gsum
    top_g = jnp.min(jnp.where(lg == gmax, lane, big), axis=1, keepdims=True)

    expert = lane - EXPERT_LANE0
    in_group = (expert >= 0) & (expert < N_EXPERTS) & ((expert >> 3) == top_g)
    le = jnp.where(in_group, logits, neg_inf)
    m1 = jnp.max(le, axis=1, keepdims=True)
    i1 = jnp.min(jnp.where(le == m1, lane, big), axis=1, keepdims=True)
    le2 = jnp.where(lane == i1, neg_inf, le)
    m2 = jnp.max(le2, axis=1, keepdims=True)
    i2 = jnp.min(jnp.where(le2 == m2, lane, big), axis=1, keepdims=True)
    r = jnp.exp(m2 - m1)
    w1 = p_top / (1.0 + r)
    w2 = p_top * r / (1.0 + r)

    @pl.when(is_first)
    def _():
        base_ref[...] = jnp.zeros_like(base_ref)

    hit1 = lane == i1
    hit2 = lane == i2
    used = (hit1 | hit2).astype(F32)
    before = _dot(ltri_ref[...], used.astype(BF16)) + base_ref[0:1, :]
    rank1 = jnp.sum(jnp.where(hit1, before, 0.0), axis=1, keepdims=True)
    rank2 = jnp.sum(jnp.where(hit2, before, 0.0), axis=1, keepdims=True)
    new_base = base_ref[0:1, :] + jnp.sum(used, axis=0, keepdims=True)
    base_ref[...] = jnp.broadcast_to(new_base, base_ref.shape)
    counts_ref[...] = jnp.broadcast_to(new_base, counts_ref.shape)

    e1 = (i1 - EXPERT_LANE0).astype(F32)
    e2 = (i2 - EXPERT_LANE0).astype(F32)
    info = jnp.zeros((tm, ROUTER_LANES), F32)
    for idx, val in enumerate((e1, e2, w1, w2, rank1, rank2)):
        info = jnp.where(lane == idx, val, info)
    rinfo_ref[...] = info


def _router_specs(tm, index):
    const = lambda shape: pl.BlockSpec(shape, lambda *g: (0,) * len(shape))
    in_specs = [const((1, D_MODEL)), const((D_MODEL, ROUTER_LANES)), const((D_MODEL, ROUTER_LANES)),
                const((tm, tm))]
    out_specs = [pl.BlockSpec((tm, D_MODEL), index), pl.BlockSpec((tm, ROUTER_LANES), index),
                 const((8, ROUTER_LANES))]
    return in_specs, out_specs


def _router_out_shapes(t):
    return [jax.ShapeDtypeStruct((t, D_MODEL), F32), jax.ShapeDtypeStruct((t, ROUTER_LANES), F32),
            jax.ShapeDtypeStruct((8, ROUTER_LANES), F32)]


def _even_outproj_kernel(x_ref, mix_ref, wout_ref, gain_ref, wr_hi_ref, wr_lo_ref, ltri_ref,
                         x1_ref, xn_ref, rinfo_ref, counts_ref, base_ref):
    x1 = x_ref[...] + _dot(mix_ref[...], wout_ref[...])
    x1_ref[...] = x1
    _router_epilogue(x1, gain_ref, wr_hi_ref, wr_lo_ref, ltri_ref, base_ref, pl.program_id(0) == 0,
                     xn_ref, rinfo_ref, counts_ref)


def _even_outproj(x, mix, w_out, gain, wr_hi, wr_lo, ltri):
    t = x.shape[0]
    index = lambda i: (i, 0)
    r_in, r_out = _router_specs(TM_PROJ, index)
    return pl.pallas_call(
        _even_outproj_kernel,
        grid=(t // TM_PROJ,),
        in_specs=[pl.BlockSpec((TM_PROJ, D_MODEL), index), pl.BlockSpec((TM_PROJ, D_MODEL), index),
                  pl.BlockSpec((D_MODEL, D_MODEL), lambda i: (0, 0))] + r_in,
        out_specs=[pl.BlockSpec((TM_PROJ, D_MODEL), index)] + r_out,
        out_shape=[jax.ShapeDtypeStruct((t, D_MODEL), F32)] + _router_out_shapes(t),
        scratch_shapes=[pltpu.VMEM((8, ROUTER_LANES), F32)],
        compiler_params=pltpu.CompilerParams(dimension_semantics=("arbitrary",), vmem_limit_bytes=VMEM_LIMIT),
        name="even_outproj_router",
    )(x, mix, w_out, gain, wr_hi, wr_lo, ltri)


CONV_PAD = 8


def _odd_mixer_kernel(x_ref, gain_mix_ref, win_ref, convw_ref, wout_ref, gain_ref, wr_hi_ref, wr_lo_ref,
                      ltri_ref, x1_ref, xn_ref, rinfo_ref, counts_ref, base_ref, ybuf_ref):
    s = pl.program_id(1)
    tm = x_ref.shape[0]
    x = x_ref[...]
    h = _rms_norm(x, gain_mix_ref[...]).astype(BF16)
    gate_b = _dot(h, win_ref[:, 0:D_MODEL])
    gate_c = _dot(h, win_ref[:, D_MODEL:2 * D_MODEL])
    y = gate_c * _dot(h, win_ref[:, 2 * D_MODEL:3 * D_MODEL])

    @pl.when(s == 0)
    def _():
        ybuf_ref[0:CONV_PAD, :] = jnp.zeros((CONV_PAD, D_MODEL), F32)

    ybuf_ref[CONV_PAD:CONV_PAD + tm, :] = y
    y1 = ybuf_ref[CONV_PAD - 1:CONV_PAD - 1 + tm, :]
    y2 = ybuf_ref[CONV_PAD - 2:CONV_PAD - 2 + tm, :]
    conv = convw_ref[0:1, :] * y2 + convw_ref[1:2, :] * y1 + convw_ref[2:3, :] * y
    ybuf_ref[0:CONV_PAD, :] = y[tm - CONV_PAD:tm, :]
    x1 = x + _dot((gate_b * conv).astype(BF16), wout_ref[...])
    x1_ref[...] = x1
    _router_epilogue(x1, gain_ref, wr_hi_ref, wr_lo_ref, ltri_ref, base_ref,
                     (pl.program_id(0) == 0) & (s == 0), xn_ref, rinfo_ref, counts_ref)


def _odd_mixer(x, gain_mix, w_in, conv_w, w_out, gain, wr_hi, wr_lo, ltri, batch, seq):
    t = x.shape[0]
    ns = seq // TM_PROJ
    index = lambda b, s: (b * ns + s, 0)
    const = lambda shape: pl.BlockSpec(shape, lambda b, s: (0,) * len(shape))
    r_in, r_out = _router_specs(TM_PROJ, index)
    return pl.pallas_call(
        _odd_mixer_kernel,
        grid=(batch, ns),
        in_specs=[pl.BlockSpec((TM_PROJ, D_MODEL), index), const((1, D_MODEL)),
                  const((D_MODEL, 3 * D_MODEL)), const((3, D_MODEL)), const((D_MODEL, D_MODEL))] + r_in,
        out_specs=[pl.BlockSpec((TM_PROJ, D_MODEL), index)] + r_out,
        out_shape=[jax.ShapeDtypeStruct((t, D_MODEL), F32)] + _router_out_shapes(t),
        scratch_shapes=[pltpu.VMEM((8, ROUTER_LANES), F32),
                        pltpu.VMEM((CONV_PAD + TM_PROJ, D_MODEL), F32)],
        compiler_params=pltpu.CompilerParams(dimension_semantics=("arbitrary", "arbitrary"),
                                             vmem_limit_bytes=VMEM_LIMIT),
        name="odd_mixer_router",
    )(x, gain_mix, w_in, conv_w, w_out, gain, wr_hi, wr_lo, ltri)


def _dispatch_kernel(pos_ref, xn_ref, xs_in_ref, xs_ref, sem):
    del xs_in_ref
    tm = xn_ref.shape[0]

    def copy(r, slot):
        return pltpu.make_async_copy(xn_ref.at[pl.ds(r, 1), :],
                                     xs_ref.at[pl.ds(pos_ref[2 * r + slot], 1), :], sem)

    def start(r, c):
        copy(r, 0).start()
        copy(r, 1).start()
        return c

    def wait(r, c):
        copy(r, 0).wait()
        copy(r, 1).wait()
        return c

    lax.fori_loop(0, tm, start, 0)
    lax.fori_loop(0, tm, wait, 0)


def _dispatch(pos_flat, xn, xs_init):
    t = xn.shape[0]
    return pl.pallas_call(
        _dispatch_kernel,
        grid=(t // TM_DISPATCH,),
        in_specs=[pl.BlockSpec((2 * TM_DISPATCH,), lambda i: (i,), memory_space=pltpu.SMEM),
                  pl.BlockSpec((TM_DISPATCH, D_MODEL), lambda i: (i, 0)),
                  pl.BlockSpec(memory_space=pl.ANY)],
        out_specs=pl.BlockSpec(memory_space=pl.ANY),
        out_shape=jax.ShapeDtypeStruct(xs_init.shape, xs_init.dtype),
        scratch_shapes=[pltpu.SemaphoreType.DMA(())],
        input_output_aliases={2: 0},
        compiler_params=pltpu.CompilerParams(dimension_semantics=("arbitrary",)),
        name="moe_dispatch",
    )(pos_flat, xn, xs_init)


def _expert_ffn_kernel(tile_expert_ref, n_used_ref, xs_ref, wg_ref, wu_ref, wd_ref, ys_ref):
    del tile_expert_ref
    active = pl.program_id(0) < n_used_ref[0]

    @pl.when(active)
    def _():
        x = xs_ref[...].astype(BF16)
        g = _dot(x, wg_ref[...].astype(BF16))
        up = _dot(x, wu_ref[...].astype(BF16))
        h = (g * jax.nn.sigmoid(g)) * up
        ys_ref[...] = _dot(h.astype(BF16), wd_ref[...].astype(BF16))

    @pl.when(jnp.logical_not(active))
    def _():
        ys_ref[...] = jnp.zeros_like(ys_ref)


def _expert_ffn(tile_expert, n_used, xs, w_gate, w_up, w_down, layer):
    n_tiles = xs.shape[0] // TM_MOE
    xs_index = lambda t, te, nu: (jnp.minimum(t, nu[0] - 1), 0)
    w_index = lambda t, te, nu: (layer, te[t], 0, 0)
    grid_spec = pltpu.PrefetchScalarGridSpec(
        num_scalar_prefetch=2,
        grid=(n_tiles,),
        in_specs=[pl.BlockSpec((TM_MOE, D_MODEL), xs_index),
                  pl.BlockSpec((None, None, D_MODEL, EXPERT_FF), w_index),
                  pl.BlockSpec((None, None, D_MODEL, EXPERT_FF), w_index),
                  pl.BlockSpec((None, None, EXPERT_FF, D_MODEL), w_index)],
        out_specs=pl.BlockSpec((TM_MOE, D_MODEL), lambda t, te, nu: (t, 0)),
    )
    return pl.pallas_call(
        _expert_ffn_kernel,
        grid_spec=grid_spec,
        out_shape=jax.ShapeDtypeStruct(xs.shape, F32),
        compiler_params=pltpu.CompilerParams(dimension_semantics=("arbitrary",), vmem_limit_bytes=VMEM_LIMIT),
        name="expert_ffn",
    )(tile_expert, n_used, xs, w_gate, w_up, w_down)


def _combine_kernel(pos_ref, x_ref, rinfo_ref, gain_ref, ys_ref, out_ref, buf_ref, sem, *, final_norm):
    tm = x_ref.shape[0]

    def copy(r, slot):
        return pltpu.make_async_copy(ys_ref.at[pl.ds(pos_ref[2 * r + slot], 1), :],
                                     buf_ref.at[slot, pl.ds(r, 1), :], sem)

    def start(r, c):
        copy(r, 0).start()
        copy(r, 1).start()
        return c

    def wait(r, c):
        copy(r, 0).wait()
        copy(r, 1).wait()
        return c

    lax.fori_loop(0, tm, start, 0)
    lax.fori_loop(0, tm, wait, 0)
    out = x_ref[...] + rinfo_ref[:, 2:3] * buf_ref[0] + rinfo_ref[:, 3:4] * buf_ref[1]
    if final_norm:
        out = _rms_norm(out, gain_ref[...])
    out_ref[...] = out


def _combine(pos_flat, x1, rinfo, gain, ys, final_norm):
    t = x1.shape[0]
    return pl.pallas_call(
        functools.partial(_combine_kernel, final_norm=final_norm),
        grid=(t // TM_COMBINE,),
        in_specs=[pl.BlockSpec((2 * TM_COMBINE,), lambda i: (i,), memory_space=pltpu.SMEM),
                  pl.BlockSpec((TM_COMBINE, D_MODEL), lambda i: (i, 0)),
                  pl.BlockSpec((TM_COMBINE, ROUTER_LANES), lambda i: (i, 0)),
                  pl.BlockSpec((1, D_MODEL), lambda i: (0, 0)),
                  pl.BlockSpec(memory_space=pl.ANY)],
        out_specs=pl.BlockSpec((TM_COMBINE, D_MODEL), lambda i: (i, 0)),
        out_shape=jax.ShapeDtypeStruct((t, D_MODEL), F32),
        scratch_shapes=[pltpu.VMEM((2, TM_COMBINE, D_MODEL), F32), pltpu.SemaphoreType.DMA(())],
        compiler_params=pltpu.CompilerParams(dimension_semantics=("arbitrary",), vmem_limit_bytes=VMEM_LIMIT),
        name="moe_combine",
    )(pos_flat, x1, rinfo, gain, ys)


def _moe(x1, xn, rinfo, counts, w_gate, w_up, w_down, layer, final_gain):
    t = x1.shape[0]
    n_tiles = (2 * t) // TM_MOE + N_EXPERTS
    cnt = counts[0, EXPERT_LANE0:EXPERT_LANE0 + N_EXPERTS].astype(jnp.int32)
    tiles_per_expert = (cnt + TM_MOE - 1) // TM_MOE
    tile_end = jnp.cumsum(tiles_per_expert)
    row_start = (tile_end - tiles_per_expert) * TM_MOE
    n_used = tile_end[-1:]
    expert = rinfo[:, 0:2].astype(jnp.int32)
    rank = rinfo[:, 4:6].astype(jnp.int32)
    pos_flat = (row_start[expert] + rank).reshape(-1)
    tile_ids = jnp.minimum(jnp.arange(n_tiles, dtype=jnp.int32), n_used - 1)
    tile_expert = jnp.sum(tile_ids[:, None] >= tile_end[None, :], axis=1).astype(jnp.int32)

    xs = _dispatch(pos_flat, xn, jnp.zeros((n_tiles * TM_MOE, D_MODEL), F32))
    ys = _expert_ffn(tile_expert, n_used.astype(jnp.int32), xs, w_gate, w_up, w_down, layer)
    gain = jnp.ones((1, D_MODEL), F32) if final_gain is None else final_gain.reshape(1, D_MODEL)
    return _combine(pos_flat, x1, rinfo, gain, ys, final_gain is not None)


def _router_weights(router_group, router_expert):
    w = jnp.concatenate([router_group, jnp.transpose(router_expert, (1, 0, 2)).reshape(D_MODEL, N_EXPERTS)],
                        axis=1)
    w = jnp.pad(w, ((0, 0), (0, ROUTER_LANES - w.shape[1])))
    return _split_bf16(w)


def kernel(x, even_w_in, even_w_out, gmlp_w_s, gmlp_b_s, gmlp_v_gain, odd_w_in, odd_conv_w, odd_w_out,
           norm_mix, norm_ffn, router_group, router_expert, w_gate, w_up, w_down, norm_final):
    batch, seq, _ = x.shape
    depth = norm_mix.shape[0]
    xt = x.reshape(batch * seq, D_MODEL)
    ltri = (lax.broadcasted_iota(jnp.int32, (TM_PROJ, TM_PROJ), 0)
            > lax.broadcasted_iota(jnp.int32, (TM_PROJ, TM_PROJ), 1)).astype(BF16)
    lane_group = jnp.arange(MIX_W) // HEAD_DIM
    gmat = (lane_group[:, None] == lane_group[None, :]).astype(BF16)
    for l in range(depth):
        i = l // 2
        gain_mix = norm_mix[l].reshape(1, D_MODEL)
        gain_ffn = norm_ffn[l].reshape(1, D_MODEL)
        wr_hi, wr_lo = _router_weights(router_group[l], router_expert[l])
        if l % 2 == 0:
            uv, q, k, v = _even_inproj(xt, gain_mix, even_w_in[i].astype(BF16))
            bias_full = jnp.repeat(gmlp_b_s[i].T, HEAD_DIM, axis=1)
            mix = _gate_attn(uv, q, k, v, gmlp_w_s[i], bias_full, gmlp_v_gain[i].reshape(1, MIX_W), gmat,
                             batch, seq)
            x1, xn, rinfo, counts = _even_outproj(xt, mix, even_w_out[i].astype(BF16), gain_ffn,
                                                  wr_hi, wr_lo, ltri)
        else:
            x1, xn, rinfo, counts = _odd_mixer(xt, gain_mix, odd_w_in[i].astype(BF16), odd_conv_w[i],
                                               odd_w_out[i].astype(BF16), gain_ffn, wr_hi, wr_lo, ltri,
                                               batch, seq)
        xt = _moe(x1, xn, rinfo, counts, w_gate, w_up, w_down, l,
                  norm_final if l == depth - 1 else None)
    return xt.reshape(batch, seq, D_MODEL)
```

```python
import functools

import jax
import jax.numpy as jnp
from jax import lax
from jax.experimental import pallas as pl
from jax.experimental.pallas import tpu as pltpu

F32 = jnp.float32
BF16 = jnp.bfloat16

D_MODEL = 1024
N_PAIR = 4
HEAD_DIM = 64
MIX_W = 512
CHUNK = 128
N_GROUPS = 4
EXPERTS_PER_GROUP = 8
N_EXPERTS = 32
EXPERT_FF = 512
EPS = 1e-6
LANES = 128

ROW_TILE = D_MODEL // LANES
RANK_SPAN = 16384
DMA_UNROLL = 8
ROUTER_LANES = 128
EXPERT_LANE0 = N_GROUPS

TM_PROJ = 512
TM_MOE = 256
TM_DISPATCH = 512
TM_COMBINE = 256
VMEM_LIMIT = 56 * 1024 * 1024


def _rms_norm(x, gain):
    return x * lax.rsqrt(jnp.mean(x * x, axis=-1, keepdims=True) + EPS) * gain


def _split_bf16(x):
    hi = x.astype(BF16)
    lo = (x - hi.astype(F32)).astype(BF16)
    return hi, lo


def _dot(a, b):
    return jnp.dot(a, b, preferred_element_type=F32)


def _store_row_tiles(ref, x):
    rows = x.shape[0]
    for s in range(ROW_TILE):
        ref[pl.ds(s, rows, stride=ROW_TILE), :] = x[:, s * LANES:(s + 1) * LANES]


def _load_row_tiles(ref, rows):
    return jnp.concatenate([ref[pl.ds(s, rows, stride=ROW_TILE), :] for s in range(ROW_TILE)], axis=1)


def _even_inproj_kernel(x_ref, gain_ref, w_ref, uv_ref, q_ref, k_ref, v_ref):
    h = _rms_norm(x_ref[...], gain_ref[...]).astype(BF16)
    uv = _dot(h, w_ref[:, 0:2 * MIX_W])
    uv = 0.5 * uv * (1.0 + lax.erf(uv * (0.5 ** 0.5)))
    uv_ref[...] = uv.astype(BF16)
    q_ref[...] = (_dot(h, w_ref[:, 2 * MIX_W:3 * MIX_W]) * (HEAD_DIM ** -0.5)).astype(BF16)
    k_ref[...] = _dot(h, w_ref[:, 3 * MIX_W:4 * MIX_W]).astype(BF16)
    v_ref[...] = _dot(h, w_ref[:, 4 * MIX_W:5 * MIX_W]).astype(BF16)


def _even_inproj(x, gain, w_in):
    t = x.shape[0]
    row = lambda w: pl.BlockSpec((TM_PROJ, w), lambda i: (i, 0))
    return pl.pallas_call(
        _even_inproj_kernel,
        grid=(t // TM_PROJ,),
        in_specs=[row(D_MODEL),
                  pl.BlockSpec((1, D_MODEL), lambda i: (0, 0)),
                  pl.BlockSpec((D_MODEL, 5 * MIX_W), lambda i: (0, 0))],
        out_specs=[row(2 * MIX_W), row(MIX_W), row(MIX_W), row(MIX_W)],
        out_shape=[jax.ShapeDtypeStruct((t, 2 * MIX_W), BF16)] + [jax.ShapeDtypeStruct((t, MIX_W), BF16)] * 3,
        compiler_params=pltpu.CompilerParams(dimension_semantics=("arbitrary",), vmem_limit_bytes=VMEM_LIMIT),
        name="even_inproj",
    )(x, gain, w_in)


def _gate_attn_kernel(uv_ref, q_ref, k_ref, v_ref, ws_ref, bias_ref, vgain_ref, gmat_ref, cmat_ref, out_ref,
                      q2_ref, carry_ref, acc_ref):
    i = pl.program_id(1)
    row = lax.broadcasted_iota(jnp.int32, (CHUNK, CHUNK), 0)
    col = lax.broadcasted_iota(jnp.int32, (CHUNK, CHUNK), 1)
    first_half = col < HEAD_DIM

    u = uv_ref[:, 0:MIX_W].astype(F32)
    v = uv_ref[:, MIX_W:2 * MIX_W].astype(F32)
    sq_hi, sq_lo = _split_bf16(v * v)
    mean_sq = (_dot(sq_hi, gmat_ref[...]) + _dot(sq_lo, gmat_ref[...])) * (1.0 / HEAD_DIM)
    vn = (v * lax.rsqrt(mean_sq + EPS) * vgain_ref[...]).astype(BF16)
    tril = row >= col
    for p in range(N_PAIR):
        cols = slice(p * LANES, (p + 1) * LANES)
        vp = vn[:, cols]
        m0 = _dot(jnp.where(tril, ws_ref[2 * p], 0.0).astype(BF16), vp)
        m1 = _dot(jnp.where(tril, ws_ref[2 * p + 1], 0.0).astype(BF16), vp)
        mixed = jnp.where(first_half, m0, m1) + bias_ref[:, cols]
        out_ref[:, cols] = (u[:, cols] * mixed).astype(out_ref.dtype)

    second_half = jnp.logical_not(first_half)
    row2 = lax.broadcasted_iota(jnp.int32, (2 * CHUNK, CHUNK), 0)
    col2 = lax.broadcasted_iota(jnp.int32, (2 * CHUNK, CHUNK), 1)
    strict2 = col2 < (row2 & (CHUNK - 1))
    for p in range(N_PAIR):
        qp = q_ref[:, p * LANES:(p + 1) * LANES]
        zero = jnp.zeros_like(qp)
        q2_ref[p] = jnp.concatenate([jnp.where(first_half, qp, zero), jnp.where(second_half, qp, zero)], axis=0)

    pairs = range(N_PAIR)

    def key_block(j, first):
        rows = pl.ds(pl.multiple_of(j * CHUNK, CHUNK), CHUNK)
        pcols = [slice(p * LANES, (p + 1) * LANES) for p in pairs]
        z = [lax.dot_general(q2_ref[p], k_ref[rows, pcols[p]], (((1,), (1,)), ((), ())),
                             preferred_element_type=F32) for p in pairs]
        log_beta, sums = [], []
        for p in pairs:
            softplus_tail = jnp.log(1.0 + jnp.exp(-jnp.abs(z[p])))
            lb = jnp.minimum(z[p], 0.0) - softplus_tail
            log_keep = lb - z[p]
            if first:
                log_keep = jnp.where(strict2, log_keep, 0.0)
            keep_hi, keep_lo = _split_bf16(log_keep)
            s = _dot(jnp.concatenate([keep_hi, keep_lo], axis=0), cmat_ref[...])
            log_beta.append(lb)
            sums.append(s[0:2 * CHUNK] + s[2 * CHUNK:4 * CHUNK])
        pv = []
        for p in pairs:
            exponent = log_beta[p] + sums[p][:, 0:LANES]
            if not first:
                exponent = exponent + carry_ref[p]
            a = jnp.exp(exponent)
            if first:
                a = jnp.where(strict2, a, 0.0)
            a = a.astype(BF16)
            a_cat = jnp.concatenate([a[0:CHUNK], a[CHUNK:2 * CHUNK]], axis=1)
            vj = v_ref[rows, pcols[p]]
            zero = jnp.zeros_like(vj)
            v2 = jnp.concatenate([jnp.where(first_half, vj, zero), jnp.where(second_half, vj, zero)], axis=0)
            pv.append(_dot(a_cat, v2))
        for p in pairs:
            if first:
                acc_ref[p] = pv[p]
                carry_ref[p] = sums[p][:, LANES:2 * LANES]
            else:
                acc_ref[p] += pv[p]
                carry_ref[p] += sums[p][:, LANES:2 * LANES]

    key_block(i, True)

    def body(step, c):
        key_block(i - 1 - step, False)
        return c

    lax.fori_loop(0, i, body, 0)
    for p in range(N_PAIR):
        out_ref[:, MIX_W + p * LANES:MIX_W + (p + 1) * LANES] = acc_ref[p].astype(out_ref.dtype)


def _gate_attn(uv, q, k, v, w_s, bias_full, v_gain, gmat, cmat, batch, seq):
    nq = seq // CHUNK
    t = uv.shape[0]
    qrow = lambda w: pl.BlockSpec((CHUNK, w), lambda b, i: (b * nq + i, 0))
    kv = pl.BlockSpec((seq, MIX_W), lambda b, i: (b, 0))
    full = lambda shape: pl.BlockSpec(shape, lambda b, i: (0,) * len(shape))
    return pl.pallas_call(
        _gate_attn_kernel,
        grid=(batch, nq),
        in_specs=[qrow(2 * MIX_W), qrow(MIX_W), kv, kv,
                  full((2 * N_PAIR, CHUNK, CHUNK)), full((CHUNK, MIX_W)), full((1, MIX_W)),
                  full((MIX_W, MIX_W)), full((CHUNK, 2 * LANES))],
        out_specs=qrow(2 * MIX_W),
        out_shape=jax.ShapeDtypeStruct((t, 2 * MIX_W), BF16),
        scratch_shapes=[pltpu.VMEM((N_PAIR, 2 * CHUNK, LANES), BF16),
                        pltpu.VMEM((N_PAIR, 2 * CHUNK, LANES), F32),
                        pltpu.VMEM((N_PAIR, CHUNK, LANES), F32)],
        compiler_params=pltpu.CompilerParams(dimension_semantics=("arbitrary", "arbitrary"),
                                             vmem_limit_bytes=VMEM_LIMIT),
        name="gate_attn",
    )(uv, q, k, v, w_s, bias_full, v_gain, gmat, cmat)


def _router_epilogue(x1, gain_ref, wr_hi_ref, wr_lo_ref, ltri_ref, base_ref, is_first,
                     xn_ref, rinfo_ref, counts_ref):
    tm = x1.shape[0]
    xn = _rms_norm(x1, gain_ref[...])
    _store_row_tiles(xn_ref, xn)
    x_hi, x_lo = _split_bf16(xn)
    logits = _dot(x_hi, wr_hi_ref[...]) + _dot(x_lo, wr_hi_ref[...]) + _dot(x_hi, wr_lo_ref[...])

    lane = lax.broadcasted_iota(jnp.int32, (tm, ROUTER_LANES), 1)
    neg_inf = jnp.float32(-jnp.inf)
    big = jnp.int32(ROUTER_LANES)
    is_group = lane < N_GROUPS
    lg = jnp.where(is_group, logits, neg_inf)
    gmax = jnp.max(lg, axis=1, keepdims=True)
    gsum = jnp.sum(jnp.where(is_group, jnp.exp(logits - gmax), 0.0), axis=1, keepdims=True)
    p_top = 1.0 / gsum
    top_g = jnp.min(jnp.where(lg == gmax, lane, big), axis=1, keepdims=True)

    expert = lane - EXPERT_LANE0
    in_group = (expert >= 0) & (expert < N_EXPERTS) & ((expert >> 3) == top_g)
    le = jnp.where(in_group, logits, neg_inf)
    m1 = jnp.max(le, axis=1, keepdims=True)
    i1 = jnp.min(jnp.where(le == m1, lane, big), axis=1, keepdims=True)
    le2 = jnp.where(lane == i1, neg_inf, le)
    m2 = jnp.max(le2, axis=1, keepdims=True)
    i2 = jnp.min(jnp.where(le2 == m2, lane, big), axis=1, keepdims=True)
    r = jnp.exp(m2 - m1)
    w1 = p_top / (1.0 + r)
    w2 = p_top * r / (1.0 + r)

    @pl.when(is_first)
    def _():
        base_ref[...] = jnp.zeros_like(base_ref)

    hit1 = lane == i1
    hit2 = lane == i2
    used = (hit1 | hit2).astype(F32)
    before = _dot(ltri_ref[...], used.astype(BF16)) + base_ref[0:1, :]
    rank1 = jnp.sum(jnp.where(hit1, before, 0.0), axis=1, keepdims=True)
    rank2 = jnp.sum(jnp.where(hit2, before, 0.0), axis=1, keepdims=True)
    new_base = base_ref[0:1, :] + jnp.sum(used, axis=0, keepdims=True)
    base_ref[...] = jnp.broadcast_to(new_base, base_ref.shape)
    counts_ref[...] = jnp.broadcast_to(new_base, counts_ref.shape)

    code1 = (i1 - EXPERT_LANE0).astype(F32) * RANK_SPAN + rank1
    code2 = (i2 - EXPERT_LANE0).astype(F32) * RANK_SPAN + rank2
    info = jnp.zeros((tm, ROUTER_LANES), F32)
    for idx, val in enumerate((code1, code2, w1, w2)):
        info = jnp.where(lane == idx, val, info)
    rinfo_ref[...] = info


def _router_specs(tm, index):
    const = lambda shape: pl.BlockSpec(shape, lambda *g: (0,) * len(shape))
    in_specs = [const((1, D_MODEL)), const((D_MODEL, ROUTER_LANES)), const((D_MODEL, ROUTER_LANES)),
                const((tm, tm))]
    out_specs = [pl.BlockSpec((tm * ROW_TILE, LANES), index), pl.BlockSpec((tm, ROUTER_LANES), index),
                 const((8, ROUTER_LANES))]
    return in_specs, out_specs


def _router_out_shapes(t):
    return [jax.ShapeDtypeStruct((t * ROW_TILE, LANES), F32), jax.ShapeDtypeStruct((t, ROUTER_LANES), F32),
            jax.ShapeDtypeStruct((8, ROUTER_LANES), F32)]


def _even_outproj_kernel(x_ref, mix_ref, wout_ref, gain_ref, wr_hi_ref, wr_lo_ref, ltri_ref,
                         x1_ref, xn_ref, rinfo_ref, counts_ref, base_ref):
    x1 = x_ref[...] + _dot(mix_ref[...], wout_ref[...])
    x1_ref[...] = x1
    _router_epilogue(x1, gain_ref, wr_hi_ref, wr_lo_ref, ltri_ref, base_ref, pl.program_id(0) == 0,
                     xn_ref, rinfo_ref, counts_ref)


def _even_outproj(x, mix, w_out, gain, wr_hi, wr_lo, ltri):
    t = x.shape[0]
    index = lambda i: (i, 0)
    r_in, r_out = _router_specs(TM_PROJ, index)
    return pl.pallas_call(
        _even_outproj_kernel,
        grid=(t // TM_PROJ,),
        in_specs=[pl.BlockSpec((TM_PROJ, D_MODEL), index), pl.BlockSpec((TM_PROJ, D_MODEL), index),
                  pl.BlockSpec((D_MODEL, D_MODEL), lambda i: (0, 0))] + r_in,
        out_specs=[pl.BlockSpec((TM_PROJ, D_MODEL), index)] + r_out,
        out_shape=[jax.ShapeDtypeStruct((t, D_MODEL), F32)] + _router_out_shapes(t),
        scratch_shapes=[pltpu.VMEM((8, ROUTER_LANES), F32)],
        compiler_params=pltpu.CompilerParams(dimension_semantics=("arbitrary",), vmem_limit_bytes=VMEM_LIMIT),
        name="even_outproj_router",
    )(x, mix, w_out, gain, wr_hi, wr_lo, ltri)


CONV_PAD = 8


def _odd_mixer_kernel(x_ref, gain_mix_ref, win_ref, convw_ref, wout_ref, gain_ref, wr_hi_ref, wr_lo_ref,
                      ltri_ref, x1_ref, xn_ref, rinfo_ref, counts_ref, base_ref, ybuf_ref):
    s = pl.program_id(1)
    tm = x_ref.shape[0]
    x = x_ref[...]
    h = _rms_norm(x, gain_mix_ref[...]).astype(BF16)
    gate_b = _dot(h, win_ref[:, 0:D_MODEL])
    gate_c = _dot(h, win_ref[:, D_MODEL:2 * D_MODEL])
    y = gate_c * _dot(h, win_ref[:, 2 * D_MODEL:3 * D_MODEL])

    @pl.when(s == 0)
    def _():
        ybuf_ref[0:CONV_PAD, :] = jnp.zeros((CONV_PAD, D_MODEL), F32)

    ybuf_ref[CONV_PAD:CONV_PAD + tm, :] = y
    y1 = ybuf_ref[CONV_PAD - 1:CONV_PAD - 1 + tm, :]
    y2 = ybuf_ref[CONV_PAD - 2:CONV_PAD - 2 + tm, :]
    conv = convw_ref[0:1, :] * y2 + convw_ref[1:2, :] * y1 + convw_ref[2:3, :] * y
    ybuf_ref[0:CONV_PAD, :] = y[tm - CONV_PAD:tm, :]
    x1 = x + _dot((gate_b * conv).astype(BF16), wout_ref[...])
    x1_ref[...] = x1
    _router_epilogue(x1, gain_ref, wr_hi_ref, wr_lo_ref, ltri_ref, base_ref,
                     (pl.program_id(0) == 0) & (s == 0), xn_ref, rinfo_ref, counts_ref)


def _odd_mixer(x, gain_mix, w_in, conv_w, w_out, gain, wr_hi, wr_lo, ltri, batch, seq):
    t = x.shape[0]
    ns = seq // TM_PROJ
    index = lambda b, s: (b * ns + s, 0)
    const = lambda shape: pl.BlockSpec(shape, lambda b, s: (0,) * len(shape))
    r_in, r_out = _router_specs(TM_PROJ, index)
    return pl.pallas_call(
        _odd_mixer_kernel,
        grid=(batch, ns),
        in_specs=[pl.BlockSpec((TM_PROJ, D_MODEL), index), const((1, D_MODEL)),
                  const((D_MODEL, 3 * D_MODEL)), const((3, D_MODEL)), const((D_MODEL, D_MODEL))] + r_in,
        out_specs=[pl.BlockSpec((TM_PROJ, D_MODEL), index)] + r_out,
        out_shape=[jax.ShapeDtypeStruct((t, D_MODEL), F32)] + _router_out_shapes(t),
        scratch_shapes=[pltpu.VMEM((8, ROUTER_LANES), F32),
                        pltpu.VMEM((CONV_PAD + TM_PROJ, D_MODEL), F32)],
        compiler_params=pltpu.CompilerParams(dimension_semantics=("arbitrary", "arbitrary"),
                                             vmem_limit_bytes=VMEM_LIMIT),
        name="odd_mixer_router",
    )(x, gain_mix, w_in, conv_w, w_out, gain, wr_hi, wr_lo, ltri)


def _row_tile(ref, row):
    return ref.at[pl.ds(pl.multiple_of(row * ROW_TILE, ROW_TILE), ROW_TILE), :]


def _sorted_row(row_start_ref, code):
    return row_start_ref[code >> 14] + (code & (RANK_SPAN - 1))


def _for_each_row_dma(tm, copy):
    def start(blk, c):
        for u in range(DMA_UNROLL):
            for slot in range(2):
                copy(blk * DMA_UNROLL + u, slot).start(priority=slot)
        return c

    def wait(blk, c):
        for u in range(DMA_UNROLL):
            for slot in range(2):
                copy(blk * DMA_UNROLL + u, slot).wait()
        return c

    lax.fori_loop(0, tm // DMA_UNROLL, start, 0)
    lax.fori_loop(0, tm // DMA_UNROLL, wait, 0)


def _dispatch_kernel(row_start_ref, code_ref, xn_ref, xs_in_ref, xs_ref, sem):
    del xs_in_ref
    tm = xn_ref.shape[0] // ROW_TILE

    def copy(r, slot):
        return pltpu.make_async_copy(_row_tile(xn_ref, r),
                                     _row_tile(xs_ref, _sorted_row(row_start_ref, code_ref[2 * r + slot])), sem)

    _for_each_row_dma(tm, copy)


def _dispatch(row_start, codes, xn_rows, xs_init):
    t = xn_rows.shape[0] // ROW_TILE
    grid_spec = pltpu.PrefetchScalarGridSpec(
        num_scalar_prefetch=1,
        grid=(t // TM_DISPATCH,),
        in_specs=[pl.BlockSpec((2 * TM_DISPATCH,), lambda i, rs: (i,), memory_space=pltpu.SMEM),
                  pl.BlockSpec((TM_DISPATCH * ROW_TILE, LANES), lambda i, rs: (i, 0)),
                  pl.BlockSpec(memory_space=pl.ANY)],
        out_specs=pl.BlockSpec(memory_space=pl.ANY),
        scratch_shapes=[pltpu.SemaphoreType.DMA(())],
    )
    return pl.pallas_call(
        _dispatch_kernel,
        grid_spec=grid_spec,
        out_shape=jax.ShapeDtypeStruct(xs_init.shape, xs_init.dtype),
        input_output_aliases={3: 0},
        compiler_params=pltpu.CompilerParams(dimension_semantics=("arbitrary",)),
        name="moe_dispatch",
    )(row_start, codes, xn_rows, xs_init)


def _expert_ffn_kernel(tile_expert_ref, n_used_ref, xs_ref, wg_ref, wu_ref, wd_ref, ys_ref):
    del tile_expert_ref
    active = pl.program_id(0) < n_used_ref[0]

    @pl.when(active)
    def _():
        x = _load_row_tiles(xs_ref, TM_MOE).astype(BF16)
        g = _dot(x, wg_ref[...].astype(BF16))
        up = _dot(x, wu_ref[...].astype(BF16))
        h = (g * jax.nn.sigmoid(g)) * up
        _store_row_tiles(ys_ref, _dot(h.astype(BF16), wd_ref[...].astype(BF16)))

    @pl.when(jnp.logical_not(active))
    def _():
        ys_ref[...] = jnp.zeros_like(ys_ref)


def _expert_ffn(tile_expert, n_used, xs, w_gate, w_up, w_down, layer):
    n_tiles = xs.shape[0] // (TM_MOE * ROW_TILE)
    xs_index = lambda t, te, nu: (jnp.minimum(t, nu[0] - 1), 0)
    w_index = lambda t, te, nu: (layer, te[t], 0, 0)
    grid_spec = pltpu.PrefetchScalarGridSpec(
        num_scalar_prefetch=2,
        grid=(n_tiles,),
        in_specs=[pl.BlockSpec((TM_MOE * ROW_TILE, LANES), xs_index),
                  pl.BlockSpec((None, None, D_MODEL, EXPERT_FF), w_index),
                  pl.BlockSpec((None, None, D_MODEL, EXPERT_FF), w_index),
                  pl.BlockSpec((None, None, EXPERT_FF, D_MODEL), w_index)],
        out_specs=pl.BlockSpec((TM_MOE * ROW_TILE, LANES), lambda t, te, nu: (t, 0)),
    )
    return pl.pallas_call(
        _expert_ffn_kernel,
        grid_spec=grid_spec,
        out_shape=jax.ShapeDtypeStruct(xs.shape, F32),
        compiler_params=pltpu.CompilerParams(dimension_semantics=("arbitrary",), vmem_limit_bytes=VMEM_LIMIT),
        name="expert_ffn",
    )(tile_expert, n_used, xs, w_gate, w_up, w_down)


def _combine_kernel(row_start_ref, code_ref, x_ref, rinfo_ref, gain_ref, ys_ref, out_ref, buf0_ref, buf1_ref,
                    sem, *, final_norm):
    tm = x_ref.shape[0]
    bufs = (buf0_ref, buf1_ref)

    def copy(r, slot):
        return pltpu.make_async_copy(_row_tile(ys_ref, _sorted_row(row_start_ref, code_ref[2 * r + slot])),
                                     _row_tile(bufs[slot], r), sem)

    _for_each_row_dma(tm, copy)
    out = (x_ref[...] + rinfo_ref[:, 2:3] * _load_row_tiles(buf0_ref, tm)
           + rinfo_ref[:, 3:4] * _load_row_tiles(buf1_ref, tm))
    if final_norm:
        out = _rms_norm(out, gain_ref[...])
    out_ref[...] = out


def _combine(row_start, codes, x1, rinfo, gain, ys, final_norm):
    t = x1.shape[0]
    grid_spec = pltpu.PrefetchScalarGridSpec(
        num_scalar_prefetch=1,
        grid=(t // TM_COMBINE,),
        in_specs=[pl.BlockSpec((2 * TM_COMBINE,), lambda i, rs: (i,), memory_space=pltpu.SMEM),
                  pl.BlockSpec((TM_COMBINE, D_MODEL), lambda i, rs: (i, 0)),
                  pl.BlockSpec((TM_COMBINE, ROUTER_LANES), lambda i, rs: (i, 0)),
                  pl.BlockSpec((1, D_MODEL), lambda i, rs: (0, 0)),
                  pl.BlockSpec(memory_space=pl.ANY)],
        out_specs=pl.BlockSpec((TM_COMBINE, D_MODEL), lambda i, rs: (i, 0)),
        scratch_shapes=[pltpu.VMEM((TM_COMBINE * ROW_TILE, LANES), F32),
                        pltpu.VMEM((TM_COMBINE * ROW_TILE, LANES), F32),
                        pltpu.SemaphoreType.DMA(())],
    )
    return pl.pallas_call(
        functools.partial(_combine_kernel, final_norm=final_norm),
        grid_spec=grid_spec,
        out_shape=jax.ShapeDtypeStruct((t, D_MODEL), F32),
        compiler_params=pltpu.CompilerParams(dimension_semantics=("arbitrary",), vmem_limit_bytes=VMEM_LIMIT),
        name="moe_combine",
    )(row_start, codes, x1, rinfo, gain, ys)


def _moe(x1, xn, rinfo, counts, w_gate, w_up, w_down, layer, final_gain):
    t = x1.shape[0]
    n_tiles = (2 * t) // TM_MOE + N_EXPERTS
    cnt = counts[0, EXPERT_LANE0:EXPERT_LANE0 + N_EXPERTS].astype(jnp.int32)
    tiles_per_expert = (cnt + TM_MOE - 1) // TM_MOE
    tile_end = jnp.cumsum(tiles_per_expert)
    row_start = (tile_end - tiles_per_expert) * TM_MOE
    n_used = tile_end[-1:]
    codes = rinfo[:, 0:2].astype(jnp.int32).reshape(-1)
    tile_ids = jnp.minimum(jnp.arange(n_tiles, dtype=jnp.int32), n_used - 1)
    tile_expert = jnp.sum(tile_ids[:, None] >= tile_end[None, :], axis=1).astype(jnp.int32)

    xs = _dispatch(row_start, codes, xn, jnp.zeros((n_tiles * TM_MOE * ROW_TILE, LANES), F32))
    ys = _expert_ffn(tile_expert, n_used.astype(jnp.int32), xs, w_gate, w_up, w_down, layer)
    gain = jnp.ones((1, D_MODEL), F32) if final_gain is None else final_gain.reshape(1, D_MODEL)
    return _combine(row_start, codes, x1, rinfo, gain, ys, final_gain is not None)


def _router_weights(router_group, router_expert):
    w = jnp.concatenate([router_group, jnp.transpose(router_expert, (1, 0, 2)).reshape(D_MODEL, N_EXPERTS)],
                        axis=1)
    w = jnp.pad(w, ((0, 0), (0, ROUTER_LANES - w.shape[1])))
    return _split_bf16(w)


def kernel(x, even_w_in, even_w_out, gmlp_w_s, gmlp_b_s, gmlp_v_gain, odd_w_in, odd_conv_w, odd_w_out,
           norm_mix, norm_ffn, router_group, router_expert, w_gate, w_up, w_down, norm_final):
    batch, seq, _ = x.shape
    depth = norm_mix.shape[0]
    xt = x.reshape(batch * seq, D_MODEL)
    ltri = (lax.broadcasted_iota(jnp.int32, (TM_PROJ, TM_PROJ), 0)
            > lax.broadcasted_iota(jnp.int32, (TM_PROJ, TM_PROJ), 1)).astype(BF16)
    lane_group = jnp.arange(MIX_W) // HEAD_DIM
    gmat = (lane_group[:, None] == lane_group[None, :]).astype(BF16)
    key = jnp.arange(CHUNK)
    cmat = jnp.concatenate([(key[:, None] > key[None, :]).astype(BF16), jnp.ones((CHUNK, LANES), BF16)], axis=1)
    for l in range(depth):
        i = l // 2
        gain_mix = norm_mix[l].reshape(1, D_MODEL)
        gain_ffn = norm_ffn[l].reshape(1, D_MODEL)
        wr_hi, wr_lo = _router_weights(router_group[l], router_expert[l])
        if l % 2 == 0:
            uv, q, k, v = _even_inproj(xt, gain_mix, even_w_in[i].astype(BF16))
            bias_full = jnp.repeat(gmlp_b_s[i].T, HEAD_DIM, axis=1)
            mix = _gate_attn(uv, q, k, v, gmlp_w_s[i], bias_full, gmlp_v_gain[i].reshape(1, MIX_W), gmat, cmat,
                             batch, seq)
            x1, xn, rinfo, counts = _even_outproj(xt, mix, even_w_out[i].astype(BF16), gain_ffn,
                                                  wr_hi, wr_lo, ltri)
        else:
            x1, xn, rinfo, counts = _odd_mixer(xt, gain_mix, odd_w_in[i].astype(BF16), odd_conv_w[i],
                                               odd_w_out[i].astype(BF16), gain_ffn, wr_hi, wr_lo, ltri,
                                               batch, seq)
        xt = _moe(x1, xn, rinfo, counts, w_gate, w_up, w_down, l,
                  norm_final if l == depth - 1 else None)
    return xt.reshape(batch, seq, D_MODEL)
```

```python
import functools

import jax
import jax.numpy as jnp
from jax import lax
from jax.experimental import pallas as pl
from jax.experimental.pallas import tpu as pltpu

F32 = jnp.float32
BF16 = jnp.bfloat16

D_MODEL = 1024
N_PAIR = 4
HEAD_DIM = 64
MIX_W = 512
CHUNK = 128
N_GROUPS = 4
EXPERTS_PER_GROUP = 8
N_EXPERTS = 32
EXPERT_FF = 512
EPS = 1e-6
LANES = 128

ROW_TILE = D_MODEL // LANES
RANK_SPAN = 16384
DMA_UNROLL = 8
KEY_UNROLL = 2
ROUTER_LANES = 128
EXPERT_LANE0 = N_GROUPS

TM_PROJ = 512
TM_MOE = 256
TM_DISPATCH = 512
TM_COMBINE = 512
VMEM_LIMIT = 56 * 1024 * 1024


def _rms_norm(x, gain):
    return x * lax.rsqrt(jnp.mean(x * x, axis=-1, keepdims=True) + EPS) * gain


def _split_bf16(x):
    hi = x.astype(BF16)
    lo = (x - hi.astype(F32)).astype(BF16)
    return hi, lo


def _dot(a, b):
    return jnp.dot(a, b, preferred_element_type=F32)


def _store_row_tiles(ref, x):
    rows = x.shape[0]
    for s in range(ROW_TILE):
        ref[pl.ds(s, rows, stride=ROW_TILE), :] = x[:, s * LANES:(s + 1) * LANES]


def _load_row_tiles(ref, rows):
    return jnp.concatenate([ref[pl.ds(s, rows, stride=ROW_TILE), :] for s in range(ROW_TILE)], axis=1)


def _even_inproj_kernel(x_ref, gain_ref, w_ref, uv_ref, q_ref, k_ref, v_ref):
    h = _rms_norm(x_ref[...], gain_ref[...]).astype(BF16)
    uv = _dot(h, w_ref[:, 0:2 * MIX_W])
    uv = 0.5 * uv * (1.0 + lax.erf(uv * (0.5 ** 0.5)))
    uv_ref[...] = uv.astype(BF16)
    q_ref[...] = (_dot(h, w_ref[:, 2 * MIX_W:3 * MIX_W]) * (HEAD_DIM ** -0.5)).astype(BF16)
    k_ref[...] = _dot(h, w_ref[:, 3 * MIX_W:4 * MIX_W]).astype(BF16)
    v_ref[...] = _dot(h, w_ref[:, 4 * MIX_W:5 * MIX_W]).astype(BF16)


def _even_inproj(x, gain, w_in):
    t = x.shape[0]
    row = lambda w: pl.BlockSpec((TM_PROJ, w), lambda i: (i, 0))
    return pl.pallas_call(
        _even_inproj_kernel,
        grid=(t // TM_PROJ,),
        in_specs=[row(D_MODEL),
                  pl.BlockSpec((1, D_MODEL), lambda i: (0, 0)),
                  pl.BlockSpec((D_MODEL, 5 * MIX_W), lambda i: (0, 0))],
        out_specs=[row(2 * MIX_W), row(MIX_W), row(MIX_W), row(MIX_W)],
        out_shape=[jax.ShapeDtypeStruct((t, 2 * MIX_W), BF16)] + [jax.ShapeDtypeStruct((t, MIX_W), BF16)] * 3,
        compiler_params=pltpu.CompilerParams(dimension_semantics=("arbitrary",), vmem_limit_bytes=VMEM_LIMIT),
        name="even_inproj",
    )(x, gain, w_in)


def _gate_attn_kernel(uv_ref, q_ref, k_ref, v_ref, ws_ref, bias_ref, vgain_ref, gmat_ref, cmat2_ref, out_ref,
                      kh_ref, vh_ref, carry_ref, acc_ref):
    i = pl.program_id(1)
    row = lax.broadcasted_iota(jnp.int32, (CHUNK, CHUNK), 0)
    col = lax.broadcasted_iota(jnp.int32, (CHUNK, CHUNK), 1)
    first_half = col < HEAD_DIM

    u = uv_ref[:, 0:MIX_W].astype(F32)
    v = uv_ref[:, MIX_W:2 * MIX_W].astype(F32)
    sq_hi, sq_lo = _split_bf16(v * v)
    mean_sq = (_dot(sq_hi, gmat_ref[...]) + _dot(sq_lo, gmat_ref[...])) * (1.0 / HEAD_DIM)
    vn = (v * lax.rsqrt(mean_sq + EPS) * vgain_ref[...]).astype(BF16)
    tril = row >= col
    for p in range(N_PAIR):
        cols = slice(p * LANES, (p + 1) * LANES)
        vp = vn[:, cols]
        m0 = _dot(jnp.where(tril, ws_ref[2 * p], 0.0).astype(BF16), vp)
        m1 = _dot(jnp.where(tril, ws_ref[2 * p + 1], 0.0).astype(BF16), vp)
        mixed = jnp.where(first_half, m0, m1) + bias_ref[:, cols]
        out_ref[:, cols] = (u[:, cols] * mixed).astype(out_ref.dtype)

    row2 = lax.broadcasted_iota(jnp.int32, (CHUNK, 2 * CHUNK), 0)
    col2 = lax.broadcasted_iota(jnp.int32, (CHUNK, 2 * CHUNK), 1)
    strict2 = (col2 & (CHUNK - 1)) < row2
    pairs = range(N_PAIR)
    pcols = [slice(p * LANES, (p + 1) * LANES) for p in pairs]

    @pl.when(i == 0)
    def _():
        lane = lax.broadcasted_iota(jnp.int32, (CHUNK, MIX_W), 1)
        head0 = (lane & HEAD_DIM) == 0

        def fill(blk, c):
            rows = pl.ds(pl.multiple_of(blk * CHUNK, CHUNK), CHUNK)
            for src, dst in ((k_ref, kh_ref), (v_ref, vh_ref)):
                x = src[rows, :]
                zero = jnp.zeros_like(x)
                dst[0, rows, :] = jnp.where(head0, x, zero)
                dst[1, rows, :] = jnp.where(head0, zero, x)
            return c

        lax.fori_loop(0, k_ref.shape[0] // CHUNK, fill, 0)

    def per_head_rows(ref, rows, cols):
        return jnp.concatenate([ref[0, rows, cols], ref[1, rows, cols]], axis=0)

    def key_blocks(js, first):
        assert len(js) == 1 or not first
        rows = [pl.ds(pl.multiple_of(j * CHUNK, CHUNK), CHUNK) for j in js]
        chains = [(b, p) for b in range(len(js)) for p in pairs]
        z = {(b, p): lax.dot_general(q_ref[:, pcols[p]], per_head_rows(kh_ref, rows[b], pcols[p]),
                                     (((1,), (1,)), ((), ())), preferred_element_type=F32)
             for b, p in chains}
        suffix, total = {}, {}
        for c in chains:
            sp = jnp.maximum(z[c], 0.0) + jnp.log(1.0 + jnp.exp(-jnp.abs(z[c])))
            if first:
                sp = jnp.where(strict2, sp, 0.0)
            hi, lo = _split_bf16(sp)
            lhs = jnp.concatenate([jnp.concatenate([hi[:, 0:CHUNK], lo[:, 0:CHUNK]], axis=1),
                                   jnp.concatenate([hi[:, CHUNK:], lo[:, CHUNK:]], axis=1)], axis=0)
            s = _dot(lhs, cmat2_ref[...])
            suffix[c] = jnp.concatenate([s[0:CHUNK, 0:CHUNK], s[CHUNK:, 0:CHUNK]], axis=1)
            total[c] = jnp.concatenate([s[0:CHUNK, CHUNK:], s[CHUNK:, CHUNK:]], axis=1)
        pv = {}
        for b, p in chains:
            exponent = z[b, p] - suffix[b, p]
            if not first:
                carry = carry_ref[p]
                for earlier in range(b):
                    carry = carry + total[earlier, p]
                exponent = exponent - carry
            a = jnp.exp(exponent)
            if first:
                a = jnp.where(strict2, a, 0.0)
            pv[b, p] = _dot(a.astype(BF16), per_head_rows(vh_ref, rows[b], pcols[p]))
        for p in pairs:
            pv_sum, total_sum = pv[0, p], total[0, p]
            for b in range(1, len(js)):
                pv_sum, total_sum = pv_sum + pv[b, p], total_sum + total[b, p]
            if first:
                acc_ref[p] = pv_sum
                carry_ref[p] = total_sum
            else:
                acc_ref[p] += pv_sum
                carry_ref[p] += total_sum

    key_blocks([i], True)

    def body(step, c):
        j = i - 1 - KEY_UNROLL * step
        key_blocks([j - u for u in range(KEY_UNROLL)], False)
        return c

    lax.fori_loop(0, i // KEY_UNROLL, body, 0)

    def tail(step, c):
        key_blocks([i % KEY_UNROLL - 1 - step], False)
        return c

    lax.fori_loop(0, i % KEY_UNROLL, tail, 0)

    for p in pairs:
        out_ref[:, MIX_W + p * LANES:MIX_W + (p + 1) * LANES] = acc_ref[p].astype(out_ref.dtype)


def _gate_attn(uv, q, k, v, w_s, bias_full, v_gain, gmat, cmat, batch, seq):
    nq = seq // CHUNK
    t = uv.shape[0]
    qrow = lambda w: pl.BlockSpec((CHUNK, w), lambda b, i: (b * nq + i, 0))
    kv = pl.BlockSpec((seq, MIX_W), lambda b, i: (b, 0))
    full = lambda shape: pl.BlockSpec(shape, lambda b, i: (0,) * len(shape))
    return pl.pallas_call(
        _gate_attn_kernel,
        grid=(batch, nq),
        in_specs=[qrow(2 * MIX_W), qrow(MIX_W), kv, kv,
                  full((2 * N_PAIR, CHUNK, CHUNK)), full((CHUNK, MIX_W)), full((1, MIX_W)),
                  full((MIX_W, MIX_W)), full((2 * CHUNK, 2 * LANES))],
        out_specs=qrow(2 * MIX_W),
        out_shape=jax.ShapeDtypeStruct((t, 2 * MIX_W), BF16),
        scratch_shapes=[pltpu.VMEM((2, seq, MIX_W), BF16), pltpu.VMEM((2, seq, MIX_W), BF16),
                        pltpu.VMEM((N_PAIR, CHUNK, 2 * LANES), F32),
                        pltpu.VMEM((N_PAIR, CHUNK, LANES), F32)],
        compiler_params=pltpu.CompilerParams(dimension_semantics=("arbitrary", "arbitrary"),
                                             vmem_limit_bytes=VMEM_LIMIT),
        name="gate_attn",
    )(uv, q, k, v, w_s, bias_full, v_gain, gmat, cmat)


def _router_epilogue(x1, gain_ref, wr_hi_ref, wr_lo_ref, ltri_ref, base_ref, is_first,
                     xn_ref, rinfo_ref, counts_ref):
    tm = x1.shape[0]
    xn = _rms_norm(x1, gain_ref[...])
    _store_row_tiles(xn_ref, xn)
    x_hi, x_lo = _split_bf16(xn)
    logits = _dot(x_hi, wr_hi_ref[...]) + _dot(x_lo, wr_hi_ref[...]) + _dot(x_hi, wr_lo_ref[...])

    lane = lax.broadcasted_iota(jnp.int32, (tm, ROUTER_LANES), 1)
    neg_inf = jnp.float32(-jnp.inf)
    lane_f = lane.astype(F32)
    big = jnp.float32(ROUTER_LANES)
    is_group = lane < N_GROUPS
    lg = jnp.where(is_group, logits, neg_inf)
    gmax = jnp.max(lg, axis=1, keepdims=True)
    gsum = jnp.sum(jnp.where(is_group, jnp.exp(logits - gmax), 0.0), axis=1, keepdims=True)
    p_top = 1.0 / gsum
    top_g = jnp.min(jnp.where(lg == gmax, lane_f, big), axis=1, keepdims=True)

    expert = lane - EXPERT_LANE0
    expert_group = (expert >> 3).astype(F32)
    in_group = (expert >= 0) & (expert < N_EXPERTS) & (expert_group == top_g)
    le = jnp.where(in_group, logits, neg_inf)
    m1 = jnp.max(le, axis=1, keepdims=True)
    i1 = jnp.min(jnp.where(le == m1, lane_f, big), axis=1, keepdims=True)
    le2 = jnp.where(lane_f == i1, neg_inf, le)
    m2 = jnp.max(le2, axis=1, keepdims=True)
    i2 = jnp.min(jnp.where(le2 == m2, lane_f, big), axis=1, keepdims=True)
    r = jnp.exp(m2 - m1)
    w1 = p_top / (1.0 + r)
    w2 = p_top * r / (1.0 + r)

    @pl.when(is_first)
    def _():
        base_ref[...] = jnp.zeros_like(base_ref)

    hit1 = lane_f == i1
    hit2 = lane_f == i2
    used = (hit1 | hit2).astype(F32)
    before = _dot(ltri_ref[...], used.astype(BF16)) + base_ref[0:1, :]
    rank1 = jnp.sum(jnp.where(hit1, before, 0.0), axis=1, keepdims=True)
    rank2 = jnp.sum(jnp.where(hit2, before, 0.0), axis=1, keepdims=True)
    new_base = base_ref[0:1, :] + jnp.sum(used, axis=0, keepdims=True)
    base_ref[...] = jnp.broadcast_to(new_base, base_ref.shape)
    counts_ref[...] = jnp.broadcast_to(new_base, counts_ref.shape)

    code1 = (i1 - EXPERT_LANE0) * RANK_SPAN + rank1
    code2 = (i2 - EXPERT_LANE0) * RANK_SPAN + rank2
    info = jnp.zeros((tm, ROUTER_LANES), F32)
    for idx, val in enumerate((code1, code2, w1, w2)):
        info = jnp.where(lane == idx, val, info)
    rinfo_ref[...] = info


def _router_specs(tm, index):
    const = lambda shape: pl.BlockSpec(shape, lambda *g: (0,) * len(shape))
    in_specs = [const((1, D_MODEL)), const((D_MODEL, ROUTER_LANES)), const((D_MODEL, ROUTER_LANES)),
                const((tm, tm))]
    out_specs = [pl.BlockSpec((tm * ROW_TILE, LANES), index), pl.BlockSpec((tm, ROUTER_LANES), index),
                 const((8, ROUTER_LANES))]
    return in_specs, out_specs


def _router_out_shapes(t):
    return [jax.ShapeDtypeStruct((t * ROW_TILE, LANES), F32), jax.ShapeDtypeStruct((t, ROUTER_LANES), F32),
            jax.ShapeDtypeStruct((8, ROUTER_LANES), F32)]


def _even_outproj_kernel(x_ref, mix_ref, wout_ref, gain_ref, wr_hi_ref, wr_lo_ref, ltri_ref,
                         x1_ref, xn_ref, rinfo_ref, counts_ref, base_ref):
    x1 = x_ref[...] + _dot(mix_ref[...], wout_ref[...])
    x1_ref[...] = x1
    _router_epilogue(x1, gain_ref, wr_hi_ref, wr_lo_ref, ltri_ref, base_ref, pl.program_id(0) == 0,
                     xn_ref, rinfo_ref, counts_ref)


def _even_outproj(x, mix, w_out, gain, wr_hi, wr_lo, ltri):
    t = x.shape[0]
    index = lambda i: (i, 0)
    r_in, r_out = _router_specs(TM_PROJ, index)
    return pl.pallas_call(
        _even_outproj_kernel,
        grid=(t // TM_PROJ,),
        in_specs=[pl.BlockSpec((TM_PROJ, D_MODEL), index), pl.BlockSpec((TM_PROJ, D_MODEL), index),
                  pl.BlockSpec((D_MODEL, D_MODEL), lambda i: (0, 0))] + r_in,
        out_specs=[pl.BlockSpec((TM_PROJ, D_MODEL), index)] + r_out,
        out_shape=[jax.ShapeDtypeStruct((t, D_MODEL), F32)] + _router_out_shapes(t),
        scratch_shapes=[pltpu.VMEM((8, ROUTER_LANES), F32)],
        compiler_params=pltpu.CompilerParams(dimension_semantics=("arbitrary",), vmem_limit_bytes=VMEM_LIMIT),
        name="even_outproj_router",
    )(x, mix, w_out, gain, wr_hi, wr_lo, ltri)


CONV_PAD = 8


def _odd_mixer_kernel(x_ref, gain_mix_ref, win_ref, convw_ref, wout_ref, gain_ref, wr_hi_ref, wr_lo_ref,
                      ltri_ref, x1_ref, xn_ref, rinfo_ref, counts_ref, base_ref, ybuf_ref):
    s = pl.program_id(1)
    tm = x_ref.shape[0]
    x = x_ref[...]
    h = _rms_norm(x, gain_mix_ref[...]).astype(BF16)
    gate_b = _dot(h, win_ref[:, 0:D_MODEL])
    gate_c = _dot(h, win_ref[:, D_MODEL:2 * D_MODEL])
    y = gate_c * _dot(h, win_ref[:, 2 * D_MODEL:3 * D_MODEL])

    @pl.when(s == 0)
    def _():
        ybuf_ref[0:CONV_PAD, :] = jnp.zeros((CONV_PAD, D_MODEL), F32)

    ybuf_ref[CONV_PAD:CONV_PAD + tm, :] = y
    y1 = ybuf_ref[CONV_PAD - 1:CONV_PAD - 1 + tm, :]
    y2 = ybuf_ref[CONV_PAD - 2:CONV_PAD - 2 + tm, :]
    conv = convw_ref[0:1, :] * y2 + convw_ref[1:2, :] * y1 + convw_ref[2:3, :] * y
    ybuf_ref[0:CONV_PAD, :] = y[tm - CONV_PAD:tm, :]
    x1 = x + _dot((gate_b * conv).astype(BF16), wout_ref[...])
    x1_ref[...] = x1
    _router_epilogue(x1, gain_ref, wr_hi_ref, wr_lo_ref, ltri_ref, base_ref,
                     (pl.program_id(0) == 0) & (s == 0), xn_ref, rinfo_ref, counts_ref)


def _odd_mixer(x, gain_mix, w_in, conv_w, w_out, gain, wr_hi, wr_lo, ltri, batch, seq):
    t = x.shape[0]
    ns = seq // TM_PROJ
    index = lambda b, s: (b * ns + s, 0)
    const = lambda shape: pl.BlockSpec(shape, lambda b, s: (0,) * len(shape))
    r_in, r_out = _router_specs(TM_PROJ, index)
    return pl.pallas_call(
        _odd_mixer_kernel,
        grid=(batch, ns),
        in_specs=[pl.BlockSpec((TM_PROJ, D_MODEL), index), const((1, D_MODEL)),
                  const((D_MODEL, 3 * D_MODEL)), const((3, D_MODEL)), const((D_MODEL, D_MODEL))] + r_in,
        out_specs=[pl.BlockSpec((TM_PROJ, D_MODEL), index)] + r_out,
        out_shape=[jax.ShapeDtypeStruct((t, D_MODEL), F32)] + _router_out_shapes(t),
        scratch_shapes=[pltpu.VMEM((8, ROUTER_LANES), F32),
                        pltpu.VMEM((CONV_PAD + TM_PROJ, D_MODEL), F32)],
        compiler_params=pltpu.CompilerParams(dimension_semantics=("arbitrary", "arbitrary"),
                                             vmem_limit_bytes=VMEM_LIMIT),
        name="odd_mixer_router",
    )(x, gain_mix, w_in, conv_w, w_out, gain, wr_hi, wr_lo, ltri)


def _row_tile(ref, first_row):
    return ref.at[pl.ds(pl.multiple_of(first_row, ROW_TILE), ROW_TILE), :]


def _for_each_row_dma(tm, copy):
    def start(blk, c):
        for u in range(DMA_UNROLL):
            for slot in range(2):
                copy(blk * DMA_UNROLL + u, slot).start(priority=slot)
        return c

    def wait(blk, c):
        for u in range(DMA_UNROLL):
            for slot in range(2):
                copy(blk * DMA_UNROLL + u, slot).wait()
        return c

    lax.fori_loop(0, tm // DMA_UNROLL, start, 0)
    lax.fori_loop(0, tm // DMA_UNROLL, wait, 0)


def _dispatch_kernel(last_tile_ref, n_used_ref, pos_ref, xn_ref, xs_ref, zeros_ref, sem, zero_sem):
    tm = xn_ref.shape[0] // ROW_TILE
    tile_rows = TM_MOE * ROW_TILE
    n_tiles = xs_ref.shape[0] // tile_rows

    @pl.when(pl.program_id(0) == 0)
    def _():
        zeros_ref[...] = jnp.zeros_like(zeros_ref)

        def zero_copy(tile):
            start = pl.multiple_of(tile * tile_rows, tile_rows)
            return pltpu.make_async_copy(zeros_ref, xs_ref.at[pl.ds(start, tile_rows), :], zero_sem)

        def for_each_zero_tile(fn):
            def last(e, c):
                @pl.when(last_tile_ref[e] >= 0)
                def _():
                    fn(zero_copy(last_tile_ref[e]))
                return c

            def unused(tile, c):
                fn(zero_copy(tile))
                return c

            lax.fori_loop(0, N_EXPERTS, last, 0)
            lax.fori_loop(n_used_ref[0], n_tiles, unused, 0)

        for_each_zero_tile(lambda dma: dma.start())
        for_each_zero_tile(lambda dma: dma.wait())

    def copy(r, slot):
        return pltpu.make_async_copy(_row_tile(xn_ref, r * ROW_TILE), _row_tile(xs_ref, pos_ref[2 * r + slot]), sem)

    _for_each_row_dma(tm, copy)


def _dispatch(last_tile, n_used, pos, xn_rows, n_tiles):
    t = xn_rows.shape[0] // ROW_TILE
    grid_spec = pltpu.PrefetchScalarGridSpec(
        num_scalar_prefetch=2,
        grid=(t // TM_DISPATCH,),
        in_specs=[pl.BlockSpec((2 * TM_DISPATCH,), lambda i, lt, nu: (i,), memory_space=pltpu.SMEM),
                  pl.BlockSpec((TM_DISPATCH * ROW_TILE, LANES), lambda i, lt, nu: (i, 0))],
        out_specs=pl.BlockSpec(memory_space=pl.ANY),
        scratch_shapes=[pltpu.VMEM((TM_MOE * ROW_TILE, LANES), F32),
                        pltpu.SemaphoreType.DMA(()), pltpu.SemaphoreType.DMA(())],
    )
    return pl.pallas_call(
        _dispatch_kernel,
        grid_spec=grid_spec,
        out_shape=jax.ShapeDtypeStruct((n_tiles * TM_MOE * ROW_TILE, LANES), F32),
        compiler_params=pltpu.CompilerParams(dimension_semantics=("arbitrary",)),
        name="moe_dispatch",
    )(last_tile, n_used, pos, xn_rows)


def _expert_ffn_kernel(tile_expert_ref, n_used_ref, xs_ref, wg_ref, wu_ref, wd_ref, ys_ref):
    del tile_expert_ref
    active = pl.program_id(0) < n_used_ref[0]

    @pl.when(active)
    def _():
        x = _load_row_tiles(xs_ref, TM_MOE).astype(BF16)
        g = _dot(x, wg_ref[...].astype(BF16))
        up = _dot(x, wu_ref[...].astype(BF16))
        h = (g * jax.nn.sigmoid(g)) * up
        _store_row_tiles(ys_ref, _dot(h.astype(BF16), wd_ref[...].astype(BF16)))

    @pl.when(jnp.logical_not(active))
    def _():
        ys_ref[...] = jnp.zeros_like(ys_ref)


def _expert_ffn(tile_expert, n_used, xs, w_gate, w_up, w_down, layer):
    n_tiles = xs.shape[0] // (TM_MOE * ROW_TILE)
    xs_index = lambda t, te, nu: (jnp.minimum(t, nu[0] - 1), 0)
    w_index = lambda t, te, nu: (layer, te[t], 0, 0)
    grid_spec = pltpu.PrefetchScalarGridSpec(
        num_scalar_prefetch=2,
        grid=(n_tiles,),
        in_specs=[pl.BlockSpec((TM_MOE * ROW_TILE, LANES), xs_index),
                  pl.BlockSpec((None, None, D_MODEL, EXPERT_FF), w_index),
                  pl.BlockSpec((None, None, D_MODEL, EXPERT_FF), w_index),
                  pl.BlockSpec((None, None, EXPERT_FF, D_MODEL), w_index)],
        out_specs=pl.BlockSpec((TM_MOE * ROW_TILE, LANES), lambda t, te, nu: (t, 0)),
    )
    return pl.pallas_call(
        _expert_ffn_kernel,
        grid_spec=grid_spec,
        out_shape=jax.ShapeDtypeStruct(xs.shape, F32),
        compiler_params=pltpu.CompilerParams(dimension_semantics=("arbitrary",), vmem_limit_bytes=VMEM_LIMIT),
        name="expert_ffn",
    )(tile_expert, n_used, xs, w_gate, w_up, w_down)


def _combine_kernel(pos_ref, x_ref, rinfo_ref, gain_ref, ys_ref, out_ref, buf0_ref, buf1_ref, sem, *, final_norm):
    tm = x_ref.shape[0]
    bufs = (buf0_ref, buf1_ref)

    def copy(r, slot):
        return pltpu.make_async_copy(_row_tile(ys_ref, pos_ref[2 * r + slot]),
                                     _row_tile(bufs[slot], r * ROW_TILE), sem)

    _for_each_row_dma(tm, copy)
    out = (x_ref[...] + rinfo_ref[:, 2:3] * _load_row_tiles(buf0_ref, tm)
           + rinfo_ref[:, 3:4] * _load_row_tiles(buf1_ref, tm))
    if final_norm:
        out = _rms_norm(out, gain_ref[...])
    out_ref[...] = out


def _combine(pos, x1, rinfo, gain, ys, final_norm):
    t = x1.shape[0]
    return pl.pallas_call(
        functools.partial(_combine_kernel, final_norm=final_norm),
        grid=(t // TM_COMBINE,),
        in_specs=[pl.BlockSpec((2 * TM_COMBINE,), lambda i: (i,), memory_space=pltpu.SMEM),
                  pl.BlockSpec((TM_COMBINE, D_MODEL), lambda i: (i, 0)),
                  pl.BlockSpec((TM_COMBINE, ROUTER_LANES), lambda i: (i, 0)),
                  pl.BlockSpec((1, D_MODEL), lambda i: (0, 0)),
                  pl.BlockSpec(memory_space=pl.ANY)],
        out_specs=pl.BlockSpec((TM_COMBINE, D_MODEL), lambda i: (i, 0)),
        out_shape=jax.ShapeDtypeStruct((t, D_MODEL), F32),
        scratch_shapes=[pltpu.VMEM((TM_COMBINE * ROW_TILE, LANES), F32),
                        pltpu.VMEM((TM_COMBINE * ROW_TILE, LANES), F32),
                        pltpu.SemaphoreType.DMA(())],
        compiler_params=pltpu.CompilerParams(dimension_semantics=("arbitrary",), vmem_limit_bytes=VMEM_LIMIT),
        name="moe_combine",
    )(pos, x1, rinfo, gain, ys)


def _moe(x1, xn, rinfo, counts, w_gate, w_up, w_down, layer, final_gain):
    t = x1.shape[0]
    n_tiles = (2 * t) // TM_MOE + N_EXPERTS
    cnt = counts[0, EXPERT_LANE0:EXPERT_LANE0 + N_EXPERTS].astype(jnp.int32)
    tiles_per_expert = (cnt + TM_MOE - 1) // TM_MOE
    tile_end = jnp.cumsum(tiles_per_expert)
    row_start = (tile_end - tiles_per_expert) * TM_MOE
    n_used = tile_end[-1:]
    codes = rinfo[:, 0:2].astype(jnp.int32).reshape(-1)
    expert_of = codes >> 14
    start_of = sum(jnp.where(expert_of == e, row_start[e], 0) for e in range(N_EXPERTS))
    pos = (start_of + (codes & (RANK_SPAN - 1))) * ROW_TILE
    tile_ids = jnp.minimum(jnp.arange(n_tiles, dtype=jnp.int32), n_used - 1)
    tile_expert = jnp.sum(tile_ids[:, None] >= tile_end[None, :], axis=1).astype(jnp.int32)

    last_tile = jnp.where(tiles_per_expert > 0, tile_end - 1, -1).astype(jnp.int32)
    n_used = n_used.astype(jnp.int32)
    xs = _dispatch(last_tile, n_used, pos, xn, n_tiles)
    ys = _expert_ffn(tile_expert, n_used, xs, w_gate, w_up, w_down, layer)
    gain = jnp.ones((1, D_MODEL), F32) if final_gain is None else final_gain.reshape(1, D_MODEL)
    return _combine(pos, x1, rinfo, gain, ys, final_gain is not None)


def _router_weights(router_group, router_expert):
    w = jnp.concatenate([router_group, jnp.transpose(router_expert, (1, 0, 2)).reshape(D_MODEL, N_EXPERTS)],
                        axis=1)
    w = jnp.pad(w, ((0, 0), (0, ROUTER_LANES - w.shape[1])))
    return _split_bf16(w)


def kernel(x, even_w_in, even_w_out, gmlp_w_s, gmlp_b_s, gmlp_v_gain, odd_w_in, odd_conv_w, odd_w_out,
           norm_mix, norm_ffn, router_group, router_expert, w_gate, w_up, w_down, norm_final):
    batch, seq, _ = x.shape
    depth = norm_mix.shape[0]
    xt = x.reshape(batch * seq, D_MODEL)
    ltri = (lax.broadcasted_iota(jnp.int32, (TM_PROJ, TM_PROJ), 0)
            > lax.broadcasted_iota(jnp.int32, (TM_PROJ, TM_PROJ), 1)).astype(BF16)
    lane_group = jnp.arange(MIX_W) // HEAD_DIM
    gmat = (lane_group[:, None] == lane_group[None, :]).astype(BF16)
    key = jnp.arange(CHUNK)
    cmat = jnp.concatenate([(key[:, None] >= key[None, :]).astype(BF16), jnp.ones((CHUNK, LANES), BF16)], axis=1)
    cmat = jnp.concatenate([cmat, cmat], axis=0)
    for l in range(depth):
        i = l // 2
        gain_mix = norm_mix[l].reshape(1, D_MODEL)
        gain_ffn = norm_ffn[l].reshape(1, D_MODEL)
        wr_hi, wr_lo = _router_weights(router_group[l], router_expert[l])
        if l % 2 == 0:
            uv, q, k, v = _even_inproj(xt, gain_mix, even_w_in[i].astype(BF16))
            bias_full = jnp.repeat(gmlp_b_s[i].T, HEAD_DIM, axis=1)
            mix = _gate_attn(uv, q, k, v, gmlp_w_s[i], bias_full, gmlp_v_gain[i].reshape(1, MIX_W), gmat, cmat,
                             batch, seq)
            x1, xn, rinfo, counts = _even_outproj(xt, mix, even_w_out[i].astype(BF16), gain_ffn,
                                                  wr_hi, wr_lo, ltri)
        else:
            x1, xn, rinfo, counts = _odd_mixer(xt, gain_mix, odd_w_in[i].astype(BF16), odd_conv_w[i],
                                               odd_w_out[i].astype(BF16), gain_ffn, wr_hi, wr_lo, ltri,
                                               batch, seq)
        xt = _moe(x1, xn, rinfo, counts, w_gate, w_up, w_down, l,
                  norm_final if l == depth - 1 else None)
    return xt.reshape(batch, seq, D_MODEL)
```

```python
import functools

import jax
import jax.numpy as jnp
from jax import lax
from jax.experimental import pallas as pl
from jax.experimental.pallas import tpu as pltpu

F32 = jnp.float32
BF16 = jnp.bfloat16

D_MODEL = 1024
N_PAIR = 4
HEAD_DIM = 64
MIX_W = 512
CHUNK = 128
N_GROUPS = 4
EXPERTS_PER_GROUP = 8
N_EXPERTS = 32
EXPERT_FF = 512
EPS = 1e-6
LANES = 128

ROW_TILE = D_MODEL // LANES
RANK_SPAN = 16384
DMA_UNROLL = 8
KEY_UNROLL = 2
ROUTER_LANES = 128
EXPERT_LANE0 = N_GROUPS

TM_PROJ = 512
TM_MOE = 256
TM_DISPATCH = 512
TM_COMBINE = 512
VMEM_LIMIT = 56 * 1024 * 1024


def _rms_norm(x, gain):
    return x * lax.rsqrt(jnp.mean(x * x, axis=-1, keepdims=True) + EPS) * gain


def _split_bf16(x):
    hi = x.astype(BF16)
    lo = (x - hi.astype(F32)).astype(BF16)
    return hi, lo


def _dot(a, b):
    return jnp.dot(a, b, preferred_element_type=F32)


def _store_row_tiles(ref, x):
    rows = x.shape[0]
    for s in range(ROW_TILE):
        ref[pl.ds(s, rows, stride=ROW_TILE), :] = x[:, s * LANES:(s + 1) * LANES]


def _load_row_tiles(ref, rows):
    return jnp.concatenate([ref[pl.ds(s, rows, stride=ROW_TILE), :] for s in range(ROW_TILE)], axis=1)


def _even_inproj_kernel(x_ref, gain_ref, w_ref, uv_ref, q_ref, k_ref, v_ref):
    h = _rms_norm(x_ref[...], gain_ref[...]).astype(BF16)
    uv = _dot(h, w_ref[:, 0:2 * MIX_W])
    uv = 0.5 * uv * (1.0 + lax.erf(uv * (0.5 ** 0.5)))
    uv_ref[...] = uv.astype(BF16)
    q_ref[...] = (_dot(h, w_ref[:, 2 * MIX_W:3 * MIX_W]) * (HEAD_DIM ** -0.5)).astype(BF16)
    k_ref[...] = _dot(h, w_ref[:, 3 * MIX_W:4 * MIX_W]).astype(BF16)
    v_ref[...] = _dot(h, w_ref[:, 4 * MIX_W:5 * MIX_W]).astype(BF16)


def _even_inproj(x, gain, w_in):
    t = x.shape[0]
    row = lambda w: pl.BlockSpec((TM_PROJ, w), lambda i: (i, 0))
    return pl.pallas_call(
        _even_inproj_kernel,
        grid=(t // TM_PROJ,),
        in_specs=[row(D_MODEL),
                  pl.BlockSpec((1, D_MODEL), lambda i: (0, 0)),
                  pl.BlockSpec((D_MODEL, 5 * MIX_W), lambda i: (0, 0))],
        out_specs=[row(2 * MIX_W), row(MIX_W), row(MIX_W), row(MIX_W)],
        out_shape=[jax.ShapeDtypeStruct((t, 2 * MIX_W), BF16)] + [jax.ShapeDtypeStruct((t, MIX_W), BF16)] * 3,
        compiler_params=pltpu.CompilerParams(dimension_semantics=("arbitrary",), vmem_limit_bytes=VMEM_LIMIT),
        name="even_inproj",
    )(x, gain, w_in)


def _gate_attn_kernel(uv_ref, q_ref, k_ref, v_ref, ws_ref, bias_ref, vgain_ref, gmat_ref, cmat2_ref, out_ref,
                      kh_ref, vh_ref, carry_ref, acc_ref):
    i = pl.program_id(1)
    row = lax.broadcasted_iota(jnp.int32, (CHUNK, CHUNK), 0)
    col = lax.broadcasted_iota(jnp.int32, (CHUNK, CHUNK), 1)
    first_half = col < HEAD_DIM

    u = uv_ref[:, 0:MIX_W].astype(F32)
    v = uv_ref[:, MIX_W:2 * MIX_W].astype(F32)
    sq_hi, sq_lo = _split_bf16(v * v)
    mean_sq = (_dot(sq_hi, gmat_ref[...]) + _dot(sq_lo, gmat_ref[...])) * (1.0 / HEAD_DIM)
    vn = (v * lax.rsqrt(mean_sq + EPS) * vgain_ref[...]).astype(BF16)
    tril = row >= col
    for p in range(N_PAIR):
        cols = slice(p * LANES, (p + 1) * LANES)
        vp = vn[:, cols]
        m0 = _dot(jnp.where(tril, ws_ref[2 * p], 0.0).astype(BF16), vp)
        m1 = _dot(jnp.where(tril, ws_ref[2 * p + 1], 0.0).astype(BF16), vp)
        mixed = jnp.where(first_half, m0, m1) + bias_ref[:, cols]
        out_ref[:, cols] = (u[:, cols] * mixed).astype(out_ref.dtype)

    row2 = lax.broadcasted_iota(jnp.int32, (CHUNK, 2 * CHUNK), 0)
    col2 = lax.broadcasted_iota(jnp.int32, (CHUNK, 2 * CHUNK), 1)
    strict2 = (col2 & (CHUNK - 1)) < row2
    pairs = range(N_PAIR)
    pcols = [slice(p * LANES, (p + 1) * LANES) for p in pairs]

    @pl.when(i == 0)
    def _():
        lane = lax.broadcasted_iota(jnp.int32, (CHUNK, MIX_W), 1)
        head0 = (lane & HEAD_DIM) == 0

        def fill(blk, c):
            rows = pl.ds(pl.multiple_of(blk * CHUNK, CHUNK), CHUNK)
            for src, dst in ((k_ref, kh_ref), (v_ref, vh_ref)):
                x = src[rows, :]
                zero = jnp.zeros_like(x)
                dst[0, rows, :] = jnp.where(head0, x, zero)
                dst[1, rows, :] = jnp.where(head0, zero, x)
            return c

        lax.fori_loop(0, k_ref.shape[0] // CHUNK, fill, 0)

    def per_head_rows(ref, rows, cols):
        return jnp.concatenate([ref[0, rows, cols], ref[1, rows, cols]], axis=0)

    def key_blocks(js, first):
        assert len(js) == 1 or not first
        rows = [pl.ds(pl.multiple_of(j * CHUNK, CHUNK), CHUNK) for j in js]
        chains = [(b, p) for b in range(len(js)) for p in pairs]
        z = {(b, p): lax.dot_general(q_ref[:, pcols[p]], per_head_rows(kh_ref, rows[b], pcols[p]),
                                     (((1,), (1,)), ((), ())), preferred_element_type=F32)
             for b, p in chains}
        suffix, total = {}, {}
        for c in chains:
            sp = jnp.maximum(z[c], 0.0) + jnp.log(1.0 + jnp.exp(-jnp.abs(z[c])))
            if first:
                sp = jnp.where(strict2, sp, 0.0)
            hi, lo = _split_bf16(sp)
            lhs = jnp.concatenate([jnp.concatenate([hi[:, 0:CHUNK], lo[:, 0:CHUNK]], axis=1),
                                   jnp.concatenate([hi[:, CHUNK:], lo[:, CHUNK:]], axis=1)], axis=0)
            s = _dot(lhs, cmat2_ref[...])
            suffix[c] = jnp.concatenate([s[0:CHUNK, 0:CHUNK], s[CHUNK:, 0:CHUNK]], axis=1)
            total[c] = jnp.concatenate([s[0:CHUNK, CHUNK:], s[CHUNK:, CHUNK:]], axis=1)
        pv = {}
        for b, p in chains:
            exponent = z[b, p] - suffix[b, p]
            if not first:
                carry = carry_ref[p]
                for earlier in range(b):
                    carry = carry + total[earlier, p]
                exponent = exponent - carry
            a = jnp.exp(exponent)
            if first:
                a = jnp.where(strict2, a, 0.0)
            pv[b, p] = _dot(a.astype(BF16), per_head_rows(vh_ref, rows[b], pcols[p]))
        for p in pairs:
            pv_sum, total_sum = pv[0, p], total[0, p]
            for b in range(1, len(js)):
                pv_sum, total_sum = pv_sum + pv[b, p], total_sum + total[b, p]
            if first:
                acc_ref[p] = pv_sum
                carry_ref[p] = total_sum
            else:
                acc_ref[p] += pv_sum
                carry_ref[p] += total_sum

    key_blocks([i], True)

    def body(step, c):
        j = i - 1 - KEY_UNROLL * step
        key_blocks([j - u for u in range(KEY_UNROLL)], False)
        return c

    lax.fori_loop(0, i // KEY_UNROLL, body, 0)

    def tail(step, c):
        key_blocks([i % KEY_UNROLL - 1 - step], False)
        return c

    lax.fori_loop(0, i % KEY_UNROLL, tail, 0)

    for p in pairs:
        out_ref[:, MIX_W + p * LANES:MIX_W + (p + 1) * LANES] = acc_ref[p].astype(out_ref.dtype)


def _gate_attn(uv, q, k, v, w_s, bias_full, v_gain, gmat, cmat, batch, seq):
    nq = seq // CHUNK
    t = uv.shape[0]
    qrow = lambda w: pl.BlockSpec((CHUNK, w), lambda b, i: (b * nq + i, 0))
    kv = pl.BlockSpec((seq, MIX_W), lambda b, i: (b, 0))
    full = lambda shape: pl.BlockSpec(shape, lambda b, i: (0,) * len(shape))
    return pl.pallas_call(
        _gate_attn_kernel,
        grid=(batch, nq),
        in_specs=[qrow(2 * MIX_W), qrow(MIX_W), kv, kv,
                  full((2 * N_PAIR, CHUNK, CHUNK)), full((CHUNK, MIX_W)), full((1, MIX_W)),
                  full((MIX_W, MIX_W)), full((2 * CHUNK, 2 * LANES))],
        out_specs=qrow(2 * MIX_W),
        out_shape=jax.ShapeDtypeStruct((t, 2 * MIX_W), BF16),
        scratch_shapes=[pltpu.VMEM((2, seq, MIX_W), BF16), pltpu.VMEM((2, seq, MIX_W), BF16),
                        pltpu.VMEM((N_PAIR, CHUNK, 2 * LANES), F32),
                        pltpu.VMEM((N_PAIR, CHUNK, LANES), F32)],
        compiler_params=pltpu.CompilerParams(dimension_semantics=("arbitrary", "arbitrary"),
                                             vmem_limit_bytes=VMEM_LIMIT),
        name="gate_attn",
    )(uv, q, k, v, w_s, bias_full, v_gain, gmat, cmat)


def _router_epilogue(x1, gain_ref, wr_hi_ref, wr_lo_ref, ltri_ref, base_ref, is_first,
                     xn_ref, rinfo_ref, counts_ref):
    tm = x1.shape[0]
    xn = _rms_norm(x1, gain_ref[...])
    _store_row_tiles(xn_ref, xn)
    x_hi, x_lo = _split_bf16(xn)
    logits = _dot(x_hi, wr_hi_ref[...]) + _dot(x_lo, wr_hi_ref[...]) + _dot(x_hi, wr_lo_ref[...])

    lane = lax.broadcasted_iota(jnp.int32, (tm, ROUTER_LANES), 1)
    neg_inf = jnp.float32(-jnp.inf)
    lane_f = lane.astype(F32)
    big = jnp.float32(ROUTER_LANES)
    is_group = lane < N_GROUPS
    lg = jnp.where(is_group, logits, neg_inf)
    gmax = jnp.max(lg, axis=1, keepdims=True)
    gsum = jnp.sum(jnp.where(is_group, jnp.exp(logits - gmax), 0.0), axis=1, keepdims=True)
    p_top = 1.0 / gsum
    top_g = jnp.min(jnp.where(lg == gmax, lane_f, big), axis=1, keepdims=True)

    expert = lane - EXPERT_LANE0
    expert_group = (expert >> 3).astype(F32)
    in_group = (expert >= 0) & (expert < N_EXPERTS) & (expert_group == top_g)
    le = jnp.where(in_group, logits, neg_inf)
    m1 = jnp.max(le, axis=1, keepdims=True)
    i1 = jnp.min(jnp.where(le == m1, lane_f, big), axis=1, keepdims=True)
    le2 = jnp.where(lane_f == i1, neg_inf, le)
    m2 = jnp.max(le2, axis=1, keepdims=True)
    i2 = jnp.min(jnp.where(le2 == m2, lane_f, big), axis=1, keepdims=True)
    r = jnp.exp(m2 - m1)
    w1 = p_top / (1.0 + r)
    w2 = p_top * r / (1.0 + r)

    @pl.when(is_first)
    def _():
        base_ref[...] = jnp.zeros_like(base_ref)

    hit1 = lane_f == i1
    hit2 = lane_f == i2
    used = (hit1 | hit2).astype(F32)
    before = _dot(ltri_ref[...], used.astype(BF16)) + base_ref[0:1, :]
    rank1 = jnp.sum(jnp.where(hit1, before, 0.0), axis=1, keepdims=True)
    rank2 = jnp.sum(jnp.where(hit2, before, 0.0), axis=1, keepdims=True)
    new_base = base_ref[0:1, :] + jnp.sum(used, axis=0, keepdims=True)
    base_ref[...] = jnp.broadcast_to(new_base, base_ref.shape)
    counts_ref[...] = jnp.broadcast_to(new_base, counts_ref.shape)

    code1 = (i1 - EXPERT_LANE0) * RANK_SPAN + rank1
    code2 = (i2 - EXPERT_LANE0) * RANK_SPAN + rank2
    info = jnp.zeros((tm, ROUTER_LANES), F32)
    for idx, val in enumerate((code1, code2, w1, w2)):
        info = jnp.where(lane == idx, val, info)
    rinfo_ref[...] = info


def _router_specs(tm, index):
    const = lambda shape: pl.BlockSpec(shape, lambda *g: (0,) * len(shape))
    in_specs = [const((1, D_MODEL)), const((D_MODEL, ROUTER_LANES)), const((D_MODEL, ROUTER_LANES)),
                const((tm, tm))]
    out_specs = [pl.BlockSpec((tm * ROW_TILE, LANES), index), pl.BlockSpec((tm, ROUTER_LANES), index),
                 const((8, ROUTER_LANES))]
    return in_specs, out_specs


def _router_out_shapes(t):
    return [jax.ShapeDtypeStruct((t * ROW_TILE, LANES), F32), jax.ShapeDtypeStruct((t, ROUTER_LANES), F32),
            jax.ShapeDtypeStruct((8, ROUTER_LANES), F32)]


def _even_outproj_kernel(x_ref, mix_ref, wout_ref, gain_ref, wr_hi_ref, wr_lo_ref, ltri_ref,
                         x1_ref, xn_ref, rinfo_ref, counts_ref, base_ref):
    x1 = x_ref[...] + _dot(mix_ref[...], wout_ref[...])
    x1_ref[...] = x1
    _router_epilogue(x1, gain_ref, wr_hi_ref, wr_lo_ref, ltri_ref, base_ref, pl.program_id(0) == 0,
                     xn_ref, rinfo_ref, counts_ref)


def _even_outproj(x, mix, w_out, gain, wr_hi, wr_lo, ltri):
    t = x.shape[0]
    index = lambda i: (i, 0)
    r_in, r_out = _router_specs(TM_PROJ, index)
    return pl.pallas_call(
        _even_outproj_kernel,
        grid=(t // TM_PROJ,),
        in_specs=[pl.BlockSpec((TM_PROJ, D_MODEL), index), pl.BlockSpec((TM_PROJ, D_MODEL), index),
                  pl.BlockSpec((D_MODEL, D_MODEL), lambda i: (0, 0))] + r_in,
        out_specs=[pl.BlockSpec((TM_PROJ, D_MODEL), index)] + r_out,
        out_shape=[jax.ShapeDtypeStruct((t, D_MODEL), F32)] + _router_out_shapes(t),
        scratch_shapes=[pltpu.VMEM((8, ROUTER_LANES), F32)],
        compiler_params=pltpu.CompilerParams(dimension_semantics=("arbitrary",), vmem_limit_bytes=VMEM_LIMIT),
        name="even_outproj_router",
    )(x, mix, w_out, gain, wr_hi, wr_lo, ltri)


CONV_PAD = 8


def _odd_mixer_kernel(x_ref, gain_mix_ref, win_ref, convw_ref, wout_ref, gain_ref, wr_hi_ref, wr_lo_ref,
                      ltri_ref, x1_ref, xn_ref, rinfo_ref, counts_ref, base_ref, ybuf_ref):
    s = pl.program_id(1)
    tm = x_ref.shape[0]
    x = x_ref[...]
    h = _rms_norm(x, gain_mix_ref[...]).astype(BF16)
    gate_b = _dot(h, win_ref[:, 0:D_MODEL])
    gate_c = _dot(h, win_ref[:, D_MODEL:2 * D_MODEL])
    y = gate_c * _dot(h, win_ref[:, 2 * D_MODEL:3 * D_MODEL])

    @pl.when(s == 0)
    def _():
        ybuf_ref[0:CONV_PAD, :] = jnp.zeros((CONV_PAD, D_MODEL), F32)

    ybuf_ref[CONV_PAD:CONV_PAD + tm, :] = y
    y1 = ybuf_ref[CONV_PAD - 1:CONV_PAD - 1 + tm, :]
    y2 = ybuf_ref[CONV_PAD - 2:CONV_PAD - 2 + tm, :]
    conv = convw_ref[0:1, :] * y2 + convw_ref[1:2, :] * y1 + convw_ref[2:3, :] * y
    ybuf_ref[0:CONV_PAD, :] = y[tm - CONV_PAD:tm, :]
    x1 = x + _dot((gate_b * conv).astype(BF16), wout_ref[...])
    x1_ref[...] = x1
    _router_epilogue(x1, gain_ref, wr_hi_ref, wr_lo_ref, ltri_ref, base_ref,
                     (pl.program_id(0) == 0) & (s == 0), xn_ref, rinfo_ref, counts_ref)


def _odd_mixer(x, gain_mix, w_in, conv_w, w_out, gain, wr_hi, wr_lo, ltri, batch, seq):
    t = x.shape[0]
    ns = seq // TM_PROJ
    index = lambda b, s: (b * ns + s, 0)
    const = lambda shape: pl.BlockSpec(shape, lambda b, s: (0,) * len(shape))
    r_in, r_out = _router_specs(TM_PROJ, index)
    return pl.pallas_call(
        _odd_mixer_kernel,
        grid=(batch, ns),
        in_specs=[pl.BlockSpec((TM_PROJ, D_MODEL), index), const((1, D_MODEL)),
                  const((D_MODEL, 3 * D_MODEL)), const((3, D_MODEL)), const((D_MODEL, D_MODEL))] + r_in,
        out_specs=[pl.BlockSpec((TM_PROJ, D_MODEL), index)] + r_out,
        out_shape=[jax.ShapeDtypeStruct((t, D_MODEL), F32)] + _router_out_shapes(t),
        scratch_shapes=[pltpu.VMEM((8, ROUTER_LANES), F32),
                        pltpu.VMEM((CONV_PAD + TM_PROJ, D_MODEL), F32)],
        compiler_params=pltpu.CompilerParams(dimension_semantics=("arbitrary", "arbitrary"),
                                             vmem_limit_bytes=VMEM_LIMIT),
        name="odd_mixer_router",
    )(x, gain_mix, w_in, conv_w, w_out, gain, wr_hi, wr_lo, ltri)


def _row_tile(ref, first_row):
    return ref.at[pl.ds(pl.multiple_of(first_row, ROW_TILE), ROW_TILE), :]


def _for_each_row_dma(tm, copy):
    def start(blk, c):
        for u in range(DMA_UNROLL):
            for slot in range(2):
                copy(blk * DMA_UNROLL + u, slot).start(priority=slot)
        return c

    def wait(blk, c):
        for u in range(DMA_UNROLL):
            for slot in range(2):
                copy(blk * DMA_UNROLL + u, slot).wait()
        return c

    lax.fori_loop(0, tm // DMA_UNROLL, start, 0)
    lax.fori_loop(0, tm // DMA_UNROLL, wait, 0)


def _sorted_first_row(row_start_ref, code):
    return (row_start_ref[code >> 14] + (code & (RANK_SPAN - 1))) * ROW_TILE


def _dispatch_kernel(row_start_ref, last_tile_ref, n_used_ref, code_ref, xn_ref, xs_ref, zeros_ref, sem, zero_sem):
    tm = xn_ref.shape[0] // ROW_TILE
    tile_rows = TM_MOE * ROW_TILE
    n_tiles = xs_ref.shape[0] // tile_rows

    @pl.when(pl.program_id(0) == 0)
    def _():
        zeros_ref[...] = jnp.zeros_like(zeros_ref)

        def zero_copy(tile):
            start = pl.multiple_of(tile * tile_rows, tile_rows)
            return pltpu.make_async_copy(zeros_ref, xs_ref.at[pl.ds(start, tile_rows), :], zero_sem)

        def for_each_zero_tile(fn):
            def last(e, c):
                @pl.when(last_tile_ref[e] >= 0)
                def _():
                    fn(zero_copy(last_tile_ref[e]))
                return c

            def unused(tile, c):
                fn(zero_copy(tile))
                return c

            lax.fori_loop(0, N_EXPERTS, last, 0)
            lax.fori_loop(n_used_ref[0], n_tiles, unused, 0)

        for_each_zero_tile(lambda dma: dma.start())
        for_each_zero_tile(lambda dma: dma.wait())

    def copy(r, slot):
        return pltpu.make_async_copy(_row_tile(xn_ref, r * ROW_TILE),
                                     _row_tile(xs_ref, _sorted_first_row(row_start_ref, code_ref[2 * r + slot])), sem)

    _for_each_row_dma(tm, copy)


def _dispatch(row_start, last_tile, n_used, codes, xn_rows, n_tiles):
    t = xn_rows.shape[0] // ROW_TILE
    grid_spec = pltpu.PrefetchScalarGridSpec(
        num_scalar_prefetch=3,
        grid=(t // TM_DISPATCH,),
        in_specs=[pl.BlockSpec((2 * TM_DISPATCH,), lambda i, rs, lt, nu: (i,), memory_space=pltpu.SMEM),
                  pl.BlockSpec((TM_DISPATCH * ROW_TILE, LANES), lambda i, rs, lt, nu: (i, 0))],
        out_specs=pl.BlockSpec(memory_space=pl.ANY),
        scratch_shapes=[pltpu.VMEM((TM_MOE * ROW_TILE, LANES), F32),
                        pltpu.SemaphoreType.DMA(()), pltpu.SemaphoreType.DMA(())],
    )
    return pl.pallas_call(
        _dispatch_kernel,
        grid_spec=grid_spec,
        out_shape=jax.ShapeDtypeStruct((n_tiles * TM_MOE * ROW_TILE, LANES), F32),
        compiler_params=pltpu.CompilerParams(dimension_semantics=("arbitrary",)),
        name="moe_dispatch",
    )(row_start, last_tile, n_used, codes, xn_rows)


def _expert_ffn_kernel(tile_start_ref, n_used_ref, xs_ref, wg_ref, wu_ref, wd_ref, ys_ref,
                       xbuf_ref, ybuf_ref, wg_bf_ref, wu_bf_ref, wd_bf_ref, in_sem, out_sem):
    e = pl.program_id(0)
    n_used = n_used_ref[0]
    first, last = tile_start_ref[e], tile_start_ref[e + 1]
    tile_rows = TM_MOE * ROW_TILE
    n_tiles = ys_ref.shape[0] // tile_rows

    def rows_of(tile):
        return pl.ds(pl.multiple_of(tile * tile_rows, tile_rows), tile_rows)

    def xs_copy(tile, slot):
        return pltpu.make_async_copy(xs_ref.at[rows_of(tile), :], xbuf_ref.at[slot], in_sem.at[slot])

    def ys_copy(tile, slot):
        return pltpu.make_async_copy(ybuf_ref.at[slot], ys_ref.at[rows_of(tile), :], out_sem.at[slot])

    @pl.when(e == 0)
    def _():
        xs_copy(0, 0).start()

    @pl.when(last > first)
    def _():
        wg_bf_ref[...] = wg_ref[...].astype(BF16)
        wu_bf_ref[...] = wu_ref[...].astype(BF16)
        wd_bf_ref[...] = wd_ref[...].astype(BF16)

    def tile_body(tile, c):
        slot = tile & 1
        xs_copy(tile, slot).wait()

        @pl.when(tile + 1 < n_used)
        def _():
            xs_copy(tile + 1, 1 - slot).start()

        x = _load_row_tiles(xbuf_ref.at[slot], TM_MOE).astype(BF16)
        g = _dot(x, wg_bf_ref[...])
        up = _dot(x, wu_bf_ref[...])
        h = (g * jax.nn.sigmoid(g)) * up
        y = _dot(h.astype(BF16), wd_bf_ref[...])

        @pl.when(tile >= 2)
        def _():
            ys_copy(tile - 2, slot).wait()

        _store_row_tiles(ybuf_ref.at[slot], y)
        ys_copy(tile, slot).start()
        return c

    lax.fori_loop(first, last, tile_body, 0)

    @pl.when(e == pl.num_programs(0) - 1)
    def _():
        @pl.when(n_used >= 2)
        def _():
            ys_copy(n_used - 2, n_used & 1).wait()

        ys_copy(n_used - 1, (n_used - 1) & 1).wait()
        ybuf_ref[0] = jnp.zeros(ybuf_ref.shape[1:], ybuf_ref.dtype)

        def start(tile, c):
            ys_copy(tile, 0).start()
            return c

        def wait(tile, c):
            ys_copy(tile, 0).wait()
            return c

        lax.fori_loop(n_used, n_tiles, start, 0)
        lax.fori_loop(n_used, n_tiles, wait, 0)


def _expert_ffn(tile_start, n_used, xs, w_gate, w_up, w_down, layer):
    tile_rows = TM_MOE * ROW_TILE
    w_index = lambda e, ts, nu: (layer, e, 0, 0)
    grid_spec = pltpu.PrefetchScalarGridSpec(
        num_scalar_prefetch=2,
        grid=(N_EXPERTS,),
        in_specs=[pl.BlockSpec(memory_space=pl.ANY),
                  pl.BlockSpec((None, None, D_MODEL, EXPERT_FF), w_index),
                  pl.BlockSpec((None, None, D_MODEL, EXPERT_FF), w_index),
                  pl.BlockSpec((None, None, EXPERT_FF, D_MODEL), w_index)],
        out_specs=pl.BlockSpec(memory_space=pl.ANY),
        scratch_shapes=[pltpu.VMEM((2, tile_rows, LANES), F32), pltpu.VMEM((2, tile_rows, LANES), F32),
                        pltpu.VMEM((D_MODEL, EXPERT_FF), BF16), pltpu.VMEM((D_MODEL, EXPERT_FF), BF16),
                        pltpu.VMEM((EXPERT_FF, D_MODEL), BF16),
                        pltpu.SemaphoreType.DMA((2,)), pltpu.SemaphoreType.DMA((2,))],
    )
    return pl.pallas_call(
        _expert_ffn_kernel,
        grid_spec=grid_spec,
        out_shape=jax.ShapeDtypeStruct(xs.shape, F32),
        compiler_params=pltpu.CompilerParams(dimension_semantics=("arbitrary",), vmem_limit_bytes=VMEM_LIMIT),
        name="expert_ffn",
    )(tile_start, n_used, xs, w_gate, w_up, w_down)


def _combine_kernel(row_start_ref, code_ref, x_ref, rinfo_ref, gain_ref, ys_ref, out_ref, buf0_ref, buf1_ref,
                    sem, *, final_norm):
    tm = x_ref.shape[0]
    bufs = (buf0_ref, buf1_ref)

    def copy(r, slot):
        return pltpu.make_async_copy(_row_tile(ys_ref, _sorted_first_row(row_start_ref, code_ref[2 * r + slot])),
                                     _row_tile(bufs[slot], r * ROW_TILE), sem)

    _for_each_row_dma(tm, copy)
    out = (x_ref[...] + rinfo_ref[:, 2:3] * _load_row_tiles(buf0_ref, tm)
           + rinfo_ref[:, 3:4] * _load_row_tiles(buf1_ref, tm))
    if final_norm:
        out = _rms_norm(out, gain_ref[...])
    out_ref[...] = out


def _combine(row_start, codes, x1, rinfo, gain, ys, final_norm):
    t = x1.shape[0]
    grid_spec = pltpu.PrefetchScalarGridSpec(
        num_scalar_prefetch=1,
        grid=(t // TM_COMBINE,),
        in_specs=[pl.BlockSpec((2 * TM_COMBINE,), lambda i, rs: (i,), memory_space=pltpu.SMEM),
                  pl.BlockSpec((TM_COMBINE, D_MODEL), lambda i, rs: (i, 0)),
                  pl.BlockSpec((TM_COMBINE, ROUTER_LANES), lambda i, rs: (i, 0)),
                  pl.BlockSpec((1, D_MODEL), lambda i, rs: (0, 0)),
                  pl.BlockSpec(memory_space=pl.ANY)],
        out_specs=pl.BlockSpec((TM_COMBINE, D_MODEL), lambda i, rs: (i, 0)),
        scratch_shapes=[pltpu.VMEM((TM_COMBINE * ROW_TILE, LANES), F32),
                        pltpu.VMEM((TM_COMBINE * ROW_TILE, LANES), F32),
                        pltpu.SemaphoreType.DMA(())],
    )
    return pl.pallas_call(
        functools.partial(_combine_kernel, final_norm=final_norm),
        grid_spec=grid_spec,
        out_shape=jax.ShapeDtypeStruct((t, D_MODEL), F32),
        compiler_params=pltpu.CompilerParams(dimension_semantics=("arbitrary",), vmem_limit_bytes=VMEM_LIMIT),
        name="moe_combine",
    )(row_start, codes, x1, rinfo, gain, ys)


def _moe(x1, xn, rinfo, counts, w_gate, w_up, w_down, layer, final_gain):
    t = x1.shape[0]
    n_tiles = (2 * t) // TM_MOE + N_EXPERTS
    cnt = counts[0, EXPERT_LANE0:EXPERT_LANE0 + N_EXPERTS].astype(jnp.int32)
    tiles_per_expert = (cnt + TM_MOE - 1) // TM_MOE
    tile_end = jnp.cumsum(tiles_per_expert)
    row_start = (tile_end - tiles_per_expert) * TM_MOE
    n_used = tile_end[-1:]
    codes = rinfo[:, 0:2].astype(jnp.int32).reshape(-1)
    tile_start = jnp.concatenate([tile_end - tiles_per_expert, n_used]).astype(jnp.int32)

    last_tile = jnp.where(tiles_per_expert > 0, tile_end - 1, -1).astype(jnp.int32)
    n_used = n_used.astype(jnp.int32)
    xs = _dispatch(row_start, last_tile, n_used, codes, xn, n_tiles)
    ys = _expert_ffn(tile_start, n_used, xs, w_gate, w_up, w_down, layer)
    gain = jnp.ones((1, D_MODEL), F32) if final_gain is None else final_gain.reshape(1, D_MODEL)
    return _combine(row_start, codes, x1, rinfo, gain, ys, final_gain is not None)


def _router_weights(router_group, router_expert):
    w = jnp.concatenate([router_group, jnp.transpose(router_expert, (1, 0, 2)).reshape(D_MODEL, N_EXPERTS)],
                        axis=1)
    w = jnp.pad(w, ((0, 0), (0, ROUTER_LANES - w.shape[1])))
    return _split_bf16(w)


def kernel(x, even_w_in, even_w_out, gmlp_w_s, gmlp_b_s, gmlp_v_gain, odd_w_in, odd_conv_w, odd_w_out,
           norm_mix, norm_ffn, router_group, router_expert, w_gate, w_up, w_down, norm_final):
    batch, seq, _ = x.shape
    depth = norm_mix.shape[0]
    xt = x.reshape(batch * seq, D_MODEL)
    ltri = (lax.broadcasted_iota(jnp.int32, (TM_PROJ, TM_PROJ), 0)
            > lax.broadcasted_iota(jnp.int32, (TM_PROJ, TM_PROJ), 1)).astype(BF16)
    lane_group = jnp.arange(MIX_W) // HEAD_DIM
    gmat = (lane_group[:, None] == lane_group[None, :]).astype(BF16)
    key = jnp.arange(CHUNK)
    cmat = jnp.concatenate([(key[:, None] >= key[None, :]).astype(BF16), jnp.ones((CHUNK, LANES), BF16)], axis=1)
    cmat = jnp.concatenate([cmat, cmat], axis=0)
    for l in range(depth):
        i = l // 2
        gain_mix = norm_mix[l].reshape(1, D_MODEL)
        gain_ffn = norm_ffn[l].reshape(1, D_MODEL)
        wr_hi, wr_lo = _router_weights(router_group[l], router_expert[l])
        if l % 2 == 0:
            uv, q, k, v = _even_inproj(xt, gain_mix, even_w_in[i].astype(BF16))
            bias_full = jnp.repeat(gmlp_b_s[i].T, HEAD_DIM, axis=1)
            mix = _gate_attn(uv, q, k, v, gmlp_w_s[i], bias_full, gmlp_v_gain[i].reshape(1, MIX_W), gmat, cmat,
                             batch, seq)
            x1, xn, rinfo, counts = _even_outproj(xt, mix, even_w_out[i].astype(BF16), gain_ffn,
                                                  wr_hi, wr_lo, ltri)
        else:
            x1, xn, rinfo, counts = _odd_mixer(xt, gain_mix, odd_w_in[i].astype(BF16), odd_conv_w[i],
                                               odd_w_out[i].astype(BF16), gain_ffn, wr_hi, wr_lo, ltri,
                                               batch, seq)
        xt = _moe(x1, xn, rinfo, counts, w_gate, w_up, w_down, l,
                  norm_final if l == depth - 1 else None)
    return xt.reshape(batch, seq, D_MODEL)
```

```python
import functools

import jax
import jax.numpy as jnp
from jax import lax
from jax.experimental import pallas as pl
from jax.experimental.pallas import tpu as pltpu

F32 = jnp.float32
BF16 = jnp.bfloat16

D_MODEL = 1024
N_PAIR = 4
HEAD_DIM = 64
MIX_W = 512
CHUNK = 128
N_GROUPS = 4
EXPERTS_PER_GROUP = 8
N_EXPERTS = 32
EXPERT_FF = 512
EPS = 1e-6
LANES = 128

ROW_TILE = D_MODEL // LANES
RANK_SPAN = 16384
DMA_UNROLL = 8
KEY_UNROLL = 2
ROUTER_LANES = 128
EXPERT_LANE0 = N_GROUPS

TM_PROJ = 512
TM_MOE = 512
TM_DISPATCH = 512
TM_COMBINE = 512
VMEM_LIMIT = 56 * 1024 * 1024


def _rms_norm(x, gain):
    return x * lax.rsqrt(jnp.mean(x * x, axis=-1, keepdims=True) + EPS) * gain


def _split_bf16(x):
    hi = x.astype(BF16)
    lo = (x - hi.astype(F32)).astype(BF16)
    return hi, lo


def _dot(a, b):
    return jnp.dot(a, b, preferred_element_type=F32)


def _store_row_tiles(ref, x):
    rows = x.shape[0]
    for s in range(ROW_TILE):
        ref[pl.ds(s, rows, stride=ROW_TILE), :] = x[:, s * LANES:(s + 1) * LANES]


def _load_row_tiles(ref, rows):
    return jnp.concatenate([ref[pl.ds(s, rows, stride=ROW_TILE), :] for s in range(ROW_TILE)], axis=1)


def _even_inproj_kernel(x_ref, gain_ref, w_ref, uv_ref, q_ref, k_ref, v_ref):
    h = _rms_norm(x_ref[...], gain_ref[...]).astype(BF16)
    uv = _dot(h, w_ref[:, 0:2 * MIX_W])
    uv = 0.5 * uv * (1.0 + lax.erf(uv * (0.5 ** 0.5)))
    uv_ref[...] = uv.astype(BF16)
    q_ref[...] = (_dot(h, w_ref[:, 2 * MIX_W:3 * MIX_W]) * (HEAD_DIM ** -0.5)).astype(BF16)
    k_ref[...] = _dot(h, w_ref[:, 3 * MIX_W:4 * MIX_W]).astype(BF16)
    v_ref[...] = _dot(h, w_ref[:, 4 * MIX_W:5 * MIX_W]).astype(BF16)


def _even_inproj(x, gain, w_in):
    t = x.shape[0]
    row = lambda w: pl.BlockSpec((TM_PROJ, w), lambda i: (i, 0))
    return pl.pallas_call(
        _even_inproj_kernel,
        grid=(t // TM_PROJ,),
        in_specs=[row(D_MODEL),
                  pl.BlockSpec((1, D_MODEL), lambda i: (0, 0)),
                  pl.BlockSpec((D_MODEL, 5 * MIX_W), lambda i: (0, 0))],
        out_specs=[row(2 * MIX_W), row(MIX_W), row(MIX_W), row(MIX_W)],
        out_shape=[jax.ShapeDtypeStruct((t, 2 * MIX_W), BF16)] + [jax.ShapeDtypeStruct((t, MIX_W), BF16)] * 3,
        compiler_params=pltpu.CompilerParams(dimension_semantics=("arbitrary",), vmem_limit_bytes=VMEM_LIMIT),
        name="even_inproj",
    )(x, gain, w_in)


def _gate_attn_kernel(uv_ref, q_ref, k_ref, v_ref, ws_ref, bias_ref, vgain_ref, gmat_ref, cmat2_ref, out_ref,
                      kh_ref, vh_ref, carry_ref, acc_ref):
    i = pl.program_id(1)
    row = lax.broadcasted_iota(jnp.int32, (CHUNK, CHUNK), 0)
    col = lax.broadcasted_iota(jnp.int32, (CHUNK, CHUNK), 1)
    first_half = col < HEAD_DIM

    u = uv_ref[:, 0:MIX_W].astype(F32)
    v = uv_ref[:, MIX_W:2 * MIX_W].astype(F32)
    sq_hi, sq_lo = _split_bf16(v * v)
    mean_sq = (_dot(sq_hi, gmat_ref[...]) + _dot(sq_lo, gmat_ref[...])) * (1.0 / HEAD_DIM)
    vn = (v * lax.rsqrt(mean_sq + EPS) * vgain_ref[...]).astype(BF16)
    tril = row >= col
    for p in range(N_PAIR):
        cols = slice(p * LANES, (p + 1) * LANES)
        vp = vn[:, cols]
        m0 = _dot(jnp.where(tril, ws_ref[2 * p], 0.0).astype(BF16), vp)
        m1 = _dot(jnp.where(tril, ws_ref[2 * p + 1], 0.0).astype(BF16), vp)
        mixed = jnp.where(first_half, m0, m1) + bias_ref[:, cols]
        out_ref[:, cols] = (u[:, cols] * mixed).astype(out_ref.dtype)

    row2 = lax.broadcasted_iota(jnp.int32, (CHUNK, 2 * CHUNK), 0)
    col2 = lax.broadcasted_iota(jnp.int32, (CHUNK, 2 * CHUNK), 1)
    strict2 = (col2 & (CHUNK - 1)) < row2
    pairs = range(N_PAIR)
    pcols = [slice(p * LANES, (p + 1) * LANES) for p in pairs]

    @pl.when(i == 0)
    def _():
        lane = lax.broadcasted_iota(jnp.int32, (CHUNK, MIX_W), 1)
        head0 = (lane & HEAD_DIM) == 0

        def fill(blk, c):
            rows = pl.ds(pl.multiple_of(blk * CHUNK, CHUNK), CHUNK)
            for src, dst in ((k_ref, kh_ref), (v_ref, vh_ref)):
                x = src[rows, :]
                zero = jnp.zeros_like(x)
                dst[0, rows, :] = jnp.where(head0, x, zero)
                dst[1, rows, :] = jnp.where(head0, zero, x)
            return c

        lax.fori_loop(0, k_ref.shape[0] // CHUNK, fill, 0)

    def per_head_rows(ref, rows, cols):
        return jnp.concatenate([ref[0, rows, cols], ref[1, rows, cols]], axis=0)

    def key_blocks(js, first):
        assert len(js) == 1 or not first
        rows = [pl.ds(pl.multiple_of(j * CHUNK, CHUNK), CHUNK) for j in js]
        chains = [(b, p) for b in range(len(js)) for p in pairs]
        z = {(b, p): lax.dot_general(q_ref[:, pcols[p]], per_head_rows(kh_ref, rows[b], pcols[p]),
                                     (((1,), (1,)), ((), ())), preferred_element_type=F32)
             for b, p in chains}
        suffix, total = {}, {}
        for c in chains:
            sp = jnp.maximum(z[c], 0.0) + jnp.log(1.0 + jnp.exp(-jnp.abs(z[c])))
            if first:
                sp = jnp.where(strict2, sp, 0.0)
            hi, lo = _split_bf16(sp)
            lhs = jnp.concatenate([jnp.concatenate([hi[:, 0:CHUNK], lo[:, 0:CHUNK]], axis=1),
                                   jnp.concatenate([hi[:, CHUNK:], lo[:, CHUNK:]], axis=1)], axis=0)
            s = _dot(lhs, cmat2_ref[...])
            suffix[c] = jnp.concatenate([s[0:CHUNK, 0:CHUNK], s[CHUNK:, 0:CHUNK]], axis=1)
            total[c] = jnp.concatenate([s[0:CHUNK, CHUNK:], s[CHUNK:, CHUNK:]], axis=1)
        pv = {}
        for b, p in chains:
            exponent = z[b, p] - suffix[b, p]
            if not first:
                carry = carry_ref[p]
                for earlier in range(b):
                    carry = carry + total[earlier, p]
                exponent = exponent - carry
            a = jnp.exp(exponent)
            if first:
                a = jnp.where(strict2, a, 0.0)
            pv[b, p] = _dot(a.astype(BF16), per_head_rows(vh_ref, rows[b], pcols[p]))
        for p in pairs:
            pv_sum, total_sum = pv[0, p], total[0, p]
            for b in range(1, len(js)):
                pv_sum, total_sum = pv_sum + pv[b, p], total_sum + total[b, p]
            if first:
                acc_ref[p] = pv_sum
                carry_ref[p] = total_sum
            else:
                acc_ref[p] += pv_sum
                carry_ref[p] += total_sum

    key_blocks([i], True)

    def body(step, c):
        j = i - 1 - KEY_UNROLL * step
        key_blocks([j - u for u in range(KEY_UNROLL)], False)
        return c

    lax.fori_loop(0, i // KEY_UNROLL, body, 0)

    def tail(step, c):
        key_blocks([i % KEY_UNROLL - 1 - step], False)
        return c

    lax.fori_loop(0, i % KEY_UNROLL, tail, 0)

    for p in pairs:
        out_ref[:, MIX_W + p * LANES:MIX_W + (p + 1) * LANES] = acc_ref[p].astype(out_ref.dtype)


def _gate_attn(uv, q, k, v, w_s, bias_full, v_gain, gmat, cmat, batch, seq):
    nq = seq // CHUNK
    t = uv.shape[0]
    qrow = lambda w: pl.BlockSpec((CHUNK, w), lambda b, i: (b * nq + i, 0))
    kv = pl.BlockSpec((seq, MIX_W), lambda b, i: (b, 0))
    full = lambda shape: pl.BlockSpec(shape, lambda b, i: (0,) * len(shape))
    return pl.pallas_call(
        _gate_attn_kernel,
        grid=(batch, nq),
        in_specs=[qrow(2 * MIX_W), qrow(MIX_W), kv, kv,
                  full((2 * N_PAIR, CHUNK, CHUNK)), full((CHUNK, MIX_W)), full((1, MIX_W)),
                  full((MIX_W, MIX_W)), full((2 * CHUNK, 2 * LANES))],
        out_specs=qrow(2 * MIX_W),
        out_shape=jax.ShapeDtypeStruct((t, 2 * MIX_W), BF16),
        scratch_shapes=[pltpu.VMEM((2, seq, MIX_W), BF16), pltpu.VMEM((2, seq, MIX_W), BF16),
                        pltpu.VMEM((N_PAIR, CHUNK, 2 * LANES), F32),
                        pltpu.VMEM((N_PAIR, CHUNK, LANES), F32)],
        compiler_params=pltpu.CompilerParams(dimension_semantics=("arbitrary", "arbitrary"),
                                             vmem_limit_bytes=VMEM_LIMIT),
        name="gate_attn",
    )(uv, q, k, v, w_s, bias_full, v_gain, gmat, cmat)


def _router_epilogue(x1, gain_ref, wr_hi_ref, wr_lo_ref, ltri_ref, base_ref, is_first,
                     xn_ref, rinfo_ref, counts_ref):
    tm = x1.shape[0]
    xn = _rms_norm(x1, gain_ref[...])
    _store_row_tiles(xn_ref, xn)
    x_hi, x_lo = _split_bf16(xn)
    logits = _dot(x_hi, wr_hi_ref[...]) + _dot(x_lo, wr_hi_ref[...]) + _dot(x_hi, wr_lo_ref[...])

    lane = lax.broadcasted_iota(jnp.int32, (tm, ROUTER_LANES), 1)
    neg_inf = jnp.float32(-jnp.inf)
    lane_f = lane.astype(F32)
    big = jnp.float32(ROUTER_LANES)
    is_group = lane < N_GROUPS
    lg = jnp.where(is_group, logits, neg_inf)
    gmax = jnp.max(lg, axis=1, keepdims=True)
    gsum = jnp.sum(jnp.where(is_group, jnp.exp(logits - gmax), 0.0), axis=1, keepdims=True)
    p_top = 1.0 / gsum
    top_g = jnp.min(jnp.where(lg == gmax, lane_f, big), axis=1, keepdims=True)

    expert = lane - EXPERT_LANE0
    expert_group = (expert >> 3).astype(F32)
    in_group = (expert >= 0) & (expert < N_EXPERTS) & (expert_group == top_g)
    le = jnp.where(in_group, logits, neg_inf)
    m1 = jnp.max(le, axis=1, keepdims=True)
    i1 = jnp.min(jnp.where(le == m1, lane_f, big), axis=1, keepdims=True)
    le2 = jnp.where(lane_f == i1, neg_inf, le)
    m2 = jnp.max(le2, axis=1, keepdims=True)
    i2 = jnp.min(jnp.where(le2 == m2, lane_f, big), axis=1, keepdims=True)
    r = jnp.exp(m2 - m1)
    w1 = p_top / (1.0 + r)
    w2 = p_top * r / (1.0 + r)

    @pl.when(is_first)
    def _():
        base_ref[...] = jnp.zeros_like(base_ref)

    hit1 = lane_f == i1
    hit2 = lane_f == i2
    used = (hit1 | hit2).astype(F32)
    before = _dot(ltri_ref[...], used.astype(BF16)) + base_ref[0:1, :]
    rank1 = jnp.sum(jnp.where(hit1, before, 0.0), axis=1, keepdims=True)
    rank2 = jnp.sum(jnp.where(hit2, before, 0.0), axis=1, keepdims=True)
    new_base = base_ref[0:1, :] + jnp.sum(used, axis=0, keepdims=True)
    base_ref[...] = jnp.broadcast_to(new_base, base_ref.shape)
    counts_ref[...] = jnp.broadcast_to(new_base, counts_ref.shape)

    code1 = (i1 - EXPERT_LANE0) * RANK_SPAN + rank1
    code2 = (i2 - EXPERT_LANE0) * RANK_SPAN + rank2
    info = jnp.zeros((tm, ROUTER_LANES), F32)
    for idx, val in enumerate((code1, code2, w1, w2)):
        info = jnp.where(lane == idx, val, info)
    rinfo_ref[...] = info


def _router_specs(tm, index):
    const = lambda shape: pl.BlockSpec(shape, lambda *g: (0,) * len(shape))
    in_specs = [const((1, D_MODEL)), const((D_MODEL, ROUTER_LANES)), const((D_MODEL, ROUTER_LANES)),
                const((tm, tm))]
    out_specs = [pl.BlockSpec((tm * ROW_TILE, LANES), index), pl.BlockSpec((tm, ROUTER_LANES), index),
                 const((8, ROUTER_LANES))]
    return in_specs, out_specs


def _router_out_shapes(t):
    return [jax.ShapeDtypeStruct((t * ROW_TILE, LANES), F32), jax.ShapeDtypeStruct((t, ROUTER_LANES), F32),
            jax.ShapeDtypeStruct((8, ROUTER_LANES), F32)]


def _even_outproj_kernel(x_ref, mix_ref, wout_ref, gain_ref, wr_hi_ref, wr_lo_ref, ltri_ref,
                         x1_ref, xn_ref, rinfo_ref, counts_ref, base_ref):
    x1 = x_ref[...] + _dot(mix_ref[...], wout_ref[...])
    x1_ref[...] = x1
    _router_epilogue(x1, gain_ref, wr_hi_ref, wr_lo_ref, ltri_ref, base_ref, pl.program_id(0) == 0,
                     xn_ref, rinfo_ref, counts_ref)


def _even_outproj(x, mix, w_out, gain, wr_hi, wr_lo, ltri):
    t = x.shape[0]
    index = lambda i: (i, 0)
    r_in, r_out = _router_specs(TM_PROJ, index)
    return pl.pallas_call(
        _even_outproj_kernel,
        grid=(t // TM_PROJ,),
        in_specs=[pl.BlockSpec((TM_PROJ, D_MODEL), index), pl.BlockSpec((TM_PROJ, D_MODEL), index),
                  pl.BlockSpec((D_MODEL, D_MODEL), lambda i: (0, 0))] + r_in,
        out_specs=[pl.BlockSpec((TM_PROJ, D_MODEL), index)] + r_out,
        out_shape=[jax.ShapeDtypeStruct((t, D_MODEL), F32)] + _router_out_shapes(t),
        scratch_shapes=[pltpu.VMEM((8, ROUTER_LANES), F32)],
        compiler_params=pltpu.CompilerParams(dimension_semantics=("arbitrary",), vmem_limit_bytes=VMEM_LIMIT),
        name="even_outproj_router",
    )(x, mix, w_out, gain, wr_hi, wr_lo, ltri)


CONV_PAD = 8


def _odd_mixer_kernel(x_ref, gain_mix_ref, win_ref, convw_ref, wout_ref, gain_ref, wr_hi_ref, wr_lo_ref,
                      ltri_ref, x1_ref, xn_ref, rinfo_ref, counts_ref, base_ref, ybuf_ref):
    s = pl.program_id(1)
    tm = x_ref.shape[0]
    x = x_ref[...]
    h = _rms_norm(x, gain_mix_ref[...]).astype(BF16)
    gate_b = _dot(h, win_ref[:, 0:D_MODEL])
    gate_c = _dot(h, win_ref[:, D_MODEL:2 * D_MODEL])
    y = gate_c * _dot(h, win_ref[:, 2 * D_MODEL:3 * D_MODEL])

    @pl.when(s == 0)
    def _():
        ybuf_ref[0:CONV_PAD, :] = jnp.zeros((CONV_PAD, D_MODEL), F32)

    ybuf_ref[CONV_PAD:CONV_PAD + tm, :] = y
    y1 = ybuf_ref[CONV_PAD - 1:CONV_PAD - 1 + tm, :]
    y2 = ybuf_ref[CONV_PAD - 2:CONV_PAD - 2 + tm, :]
    conv = convw_ref[0:1, :] * y2 + convw_ref[1:2, :] * y1 + convw_ref[2:3, :] * y
    ybuf_ref[0:CONV_PAD, :] = y[tm - CONV_PAD:tm, :]
    x1 = x + _dot((gate_b * conv).astype(BF16), wout_ref[...])
    x1_ref[...] = x1
    _router_epilogue(x1, gain_ref, wr_hi_ref, wr_lo_ref, ltri_ref, base_ref,
                     (pl.program_id(0) == 0) & (s == 0), xn_ref, rinfo_ref, counts_ref)


def _odd_mixer(x, gain_mix, w_in, conv_w, w_out, gain, wr_hi, wr_lo, ltri, batch, seq):
    t = x.shape[0]
    ns = seq // TM_PROJ
    index = lambda b, s: (b * ns + s, 0)
    const = lambda shape: pl.BlockSpec(shape, lambda b, s: (0,) * len(shape))
    r_in, r_out = _router_specs(TM_PROJ, index)
    return pl.pallas_call(
        _odd_mixer_kernel,
        grid=(batch, ns),
        in_specs=[pl.BlockSpec((TM_PROJ, D_MODEL), index), const((1, D_MODEL)),
                  const((D_MODEL, 3 * D_MODEL)), const((3, D_MODEL)), const((D_MODEL, D_MODEL))] + r_in,
        out_specs=[pl.BlockSpec((TM_PROJ, D_MODEL), index)] + r_out,
        out_shape=[jax.ShapeDtypeStruct((t, D_MODEL), F32)] + _router_out_shapes(t),
        scratch_shapes=[pltpu.VMEM((8, ROUTER_LANES), F32),
                        pltpu.VMEM((CONV_PAD + TM_PROJ, D_MODEL), F32)],
        compiler_params=pltpu.CompilerParams(dimension_semantics=("arbitrary", "arbitrary"),
                                             vmem_limit_bytes=VMEM_LIMIT),
        name="odd_mixer_router",
    )(x, gain_mix, w_in, conv_w, w_out, gain, wr_hi, wr_lo, ltri)


def _row_tile(ref, first_row):
    return ref.at[pl.ds(pl.multiple_of(first_row, ROW_TILE), ROW_TILE), :]


def _for_each_row_dma(tm, copy):
    def start(blk, c):
        for u in range(DMA_UNROLL):
            for slot in range(2):
                copy(blk * DMA_UNROLL + u, slot).start(priority=slot)
        return c

    def wait(blk, c):
        for u in range(DMA_UNROLL):
            for slot in range(2):
                copy(blk * DMA_UNROLL + u, slot).wait()
        return c

    lax.fori_loop(0, tm // DMA_UNROLL, start, 0)
    lax.fori_loop(0, tm // DMA_UNROLL, wait, 0)


def _sorted_first_row(row_start_ref, code):
    return (row_start_ref[code >> 14] + (code & (RANK_SPAN - 1))) * ROW_TILE


def _dispatch_kernel(row_start_ref, last_tile_ref, n_used_ref, code_ref, xn_ref, xs_ref, zeros_ref, sem, zero_sem):
    tm = xn_ref.shape[0] // ROW_TILE
    tile_rows = TM_MOE * ROW_TILE
    n_tiles = xs_ref.shape[0] // tile_rows

    @pl.when(pl.program_id(0) == 0)
    def _():
        zeros_ref[...] = jnp.zeros_like(zeros_ref)

        def zero_copy(tile):
            start = pl.multiple_of(tile * tile_rows, tile_rows)
            return pltpu.make_async_copy(zeros_ref, xs_ref.at[pl.ds(start, tile_rows), :], zero_sem)

        def for_each_zero_tile(fn):
            def last(e, c):
                @pl.when(last_tile_ref[e] >= 0)
                def _():
                    fn(zero_copy(last_tile_ref[e]))
                return c

            def unused(tile, c):
                fn(zero_copy(tile))
                return c

            lax.fori_loop(0, N_EXPERTS, last, 0)
            lax.fori_loop(n_used_ref[0], n_tiles, unused, 0)

        for_each_zero_tile(lambda dma: dma.start())
        for_each_zero_tile(lambda dma: dma.wait())

    def copy(r, slot):
        return pltpu.make_async_copy(_row_tile(xn_ref, r * ROW_TILE),
                                     _row_tile(xs_ref, _sorted_first_row(row_start_ref, code_ref[2 * r + slot])), sem)

    _for_each_row_dma(tm, copy)


def _dispatch(row_start, last_tile, n_used, codes, xn_rows, n_tiles):
    t = xn_rows.shape[0] // ROW_TILE
    grid_spec = pltpu.PrefetchScalarGridSpec(
        num_scalar_prefetch=3,
        grid=(t // TM_DISPATCH,),
        in_specs=[pl.BlockSpec((2 * TM_DISPATCH,), lambda i, rs, lt, nu: (i,), memory_space=pltpu.SMEM),
                  pl.BlockSpec((TM_DISPATCH * ROW_TILE, LANES), lambda i, rs, lt, nu: (i, 0))],
        out_specs=pl.BlockSpec(memory_space=pl.ANY),
        scratch_shapes=[pltpu.VMEM((TM_MOE * ROW_TILE, LANES), F32),
                        pltpu.SemaphoreType.DMA(()), pltpu.SemaphoreType.DMA(())],
    )
    return pl.pallas_call(
        _dispatch_kernel,
        grid_spec=grid_spec,
        out_shape=jax.ShapeDtypeStruct((n_tiles * TM_MOE * ROW_TILE, LANES), F32),
        compiler_params=pltpu.CompilerParams(dimension_semantics=("arbitrary",)),
        name="moe_dispatch",
    )(row_start, last_tile, n_used, codes, xn_rows)


def _expert_ffn_kernel(tile_start_ref, n_used_ref, xs_ref, wg_ref, wu_ref, wd_ref, ys_ref,
                       xbuf_ref, ybuf_ref, wg_bf_ref, wu_bf_ref, wd_bf_ref, in_sem, out_sem):
    e = pl.program_id(0)
    n_used = n_used_ref[0]
    first, last = tile_start_ref[e], tile_start_ref[e + 1]
    tile_rows = TM_MOE * ROW_TILE
    n_tiles = ys_ref.shape[0] // tile_rows

    def rows_of(tile):
        return pl.ds(pl.multiple_of(tile * tile_rows, tile_rows), tile_rows)

    def xs_copy(tile, slot):
        return pltpu.make_async_copy(xs_ref.at[rows_of(tile), :], xbuf_ref.at[slot], in_sem.at[slot])

    def ys_copy(tile, slot):
        return pltpu.make_async_copy(ybuf_ref.at[slot], ys_ref.at[rows_of(tile), :], out_sem.at[slot])

    @pl.when(e == 0)
    def _():
        xs_copy(0, 0).start()

    @pl.when(last > first)
    def _():
        wg_bf_ref[...] = wg_ref[...].astype(BF16)
        wu_bf_ref[...] = wu_ref[...].astype(BF16)
        wd_bf_ref[...] = wd_ref[...].astype(BF16)

    def tile_body(tile, c):
        slot = tile & 1
        xs_copy(tile, slot).wait()

        @pl.when(tile + 1 < n_used)
        def _():
            xs_copy(tile + 1, 1 - slot).start()

        x = _load_row_tiles(xbuf_ref.at[slot], TM_MOE).astype(BF16)
        g = _dot(x, wg_bf_ref[...])
        up = _dot(x, wu_bf_ref[...])
        h = (g * jax.nn.sigmoid(g)) * up
        y = _dot(h.astype(BF16), wd_bf_ref[...])

        @pl.when(tile >= 2)
        def _():
            ys_copy(tile - 2, slot).wait()

        _store_row_tiles(ybuf_ref.at[slot], y)
        ys_copy(tile, slot).start()
        return c

    lax.fori_loop(first, last, tile_body, 0)

    @pl.when(e == pl.num_programs(0) - 1)
    def _():
        @pl.when(n_used >= 2)
        def _():
            ys_copy(n_used - 2, n_used & 1).wait()

        ys_copy(n_used - 1, (n_used - 1) & 1).wait()
        ybuf_ref[0] = jnp.zeros(ybuf_ref.shape[1:], ybuf_ref.dtype)

        def start(tile, c):
            ys_copy(tile, 0).start()
            return c

        def wait(tile, c):
            ys_copy(tile, 0).wait()
            return c

        lax.fori_loop(n_used, n_tiles, start, 0)
        lax.fori_loop(n_used, n_tiles, wait, 0)


def _expert_ffn(tile_start, n_used, xs, w_gate, w_up, w_down, layer):
    tile_rows = TM_MOE * ROW_TILE
    w_index = lambda e, ts, nu: (layer, e, 0, 0)
    grid_spec = pltpu.PrefetchScalarGridSpec(
        num_scalar_prefetch=2,
        grid=(N_EXPERTS,),
        in_specs=[pl.BlockSpec(memory_space=pl.ANY),
                  pl.BlockSpec((None, None, D_MODEL, EXPERT_FF), w_index),
                  pl.BlockSpec((None, None, D_MODEL, EXPERT_FF), w_index),
                  pl.BlockSpec((None, None, EXPERT_FF, D_MODEL), w_index)],
        out_specs=pl.BlockSpec(memory_space=pl.ANY),
        scratch_shapes=[pltpu.VMEM((2, tile_rows, LANES), F32), pltpu.VMEM((2, tile_rows, LANES), F32),
                        pltpu.VMEM((D_MODEL, EXPERT_FF), BF16), pltpu.VMEM((D_MODEL, EXPERT_FF), BF16),
                        pltpu.VMEM((EXPERT_FF, D_MODEL), BF16),
                        pltpu.SemaphoreType.DMA((2,)), pltpu.SemaphoreType.DMA((2,))],
    )
    return pl.pallas_call(
        _expert_ffn_kernel,
        grid_spec=grid_spec,
        out_shape=jax.ShapeDtypeStruct(xs.shape, F32),
        compiler_params=pltpu.CompilerParams(dimension_semantics=("arbitrary",), vmem_limit_bytes=VMEM_LIMIT),
        name="expert_ffn",
    )(tile_start, n_used, xs, w_gate, w_up, w_down)


def _combine_kernel(row_start_ref, code_ref, x_ref, rinfo_ref, gain_ref, ys_ref, out_ref, buf0_ref, buf1_ref,
                    sem, *, final_norm):
    tm = x_ref.shape[0]
    bufs = (buf0_ref, buf1_ref)

    def copy(r, slot):
        return pltpu.make_async_copy(_row_tile(ys_ref, _sorted_first_row(row_start_ref, code_ref[2 * r + slot])),
                                     _row_tile(bufs[slot], r * ROW_TILE), sem)

    _for_each_row_dma(tm, copy)
    out = (x_ref[...] + rinfo_ref[:, 2:3] * _load_row_tiles(buf0_ref, tm)
           + rinfo_ref[:, 3:4] * _load_row_tiles(buf1_ref, tm))
    if final_norm:
        out = _rms_norm(out, gain_ref[...])
    out_ref[...] = out


def _combine(row_start, codes, x1, rinfo, gain, ys, final_norm):
    t = x1.shape[0]
    grid_spec = pltpu.PrefetchScalarGridSpec(
        num_scalar_prefetch=1,
        grid=(t // TM_COMBINE,),
        in_specs=[pl.BlockSpec((2 * TM_COMBINE,), lambda i, rs: (i,), memory_space=pltpu.SMEM),
                  pl.BlockSpec((TM_COMBINE, D_MODEL), lambda i, rs: (i, 0)),
                  pl.BlockSpec((TM_COMBINE, ROUTER_LANES), lambda i, rs: (i, 0)),
                  pl.BlockSpec((1, D_MODEL), lambda i, rs: (0, 0)),
                  pl.BlockSpec(memory_space=pl.ANY)],
        out_specs=pl.BlockSpec((TM_COMBINE, D_MODEL), lambda i, rs: (i, 0)),
        scratch_shapes=[pltpu.VMEM((TM_COMBINE * ROW_TILE, LANES), F32),
                        pltpu.VMEM((TM_COMBINE * ROW_TILE, LANES), F32),
                        pltpu.SemaphoreType.DMA(())],
    )
    return pl.pallas_call(
        functools.partial(_combine_kernel, final_norm=final_norm),
        grid_spec=grid_spec,
        out_shape=jax.ShapeDtypeStruct((t, D_MODEL), F32),
        compiler_params=pltpu.CompilerParams(dimension_semantics=("arbitrary",), vmem_limit_bytes=VMEM_LIMIT),
        name="moe_combine",
    )(row_start, codes, x1, rinfo, gain, ys)


def _moe(x1, xn, rinfo, counts, w_gate, w_up, w_down, layer, final_gain):
    t = x1.shape[0]
    n_tiles = (2 * t) // TM_MOE + N_EXPERTS
    cnt = counts[0, EXPERT_LANE0:EXPERT_LANE0 + N_EXPERTS].astype(jnp.int32)
    tiles_per_expert = (cnt + TM_MOE - 1) // TM_MOE
    tile_end = jnp.cumsum(tiles_per_expert)
    row_start = (tile_end - tiles_per_expert) * TM_MOE
    n_used = tile_end[-1:]
    codes = rinfo[:, 0:2].astype(jnp.int32).reshape(-1)
    tile_start = jnp.concatenate([tile_end - tiles_per_expert, n_used]).astype(jnp.int32)

    last_tile = jnp.where(tiles_per_expert > 0, tile_end - 1, -1).astype(jnp.int32)
    n_used = n_used.astype(jnp.int32)
    xs = _dispatch(row_start, last_tile, n_used, codes, xn, n_tiles)
    ys = _expert_ffn(tile_start, n_used, xs, w_gate, w_up, w_down, layer)
    gain = jnp.ones((1, D_MODEL), F32) if final_gain is None else final_gain.reshape(1, D_MODEL)
    return _combine(row_start, codes, x1, rinfo, gain, ys, final_gain is not None)


def _router_weights(router_group, router_expert):
    w = jnp.concatenate([router_group, jnp.transpose(router_expert, (1, 0, 2)).reshape(D_MODEL, N_EXPERTS)],
                        axis=1)
    w = jnp.pad(w, ((0, 0), (0, ROUTER_LANES - w.shape[1])))
    return _split_bf16(w)


def kernel(x, even_w_in, even_w_out, gmlp_w_s, gmlp_b_s, gmlp_v_gain, odd_w_in, odd_conv_w, odd_w_out,
           norm_mix, norm_ffn, router_group, router_expert, w_gate, w_up, w_down, norm_final):
    batch, seq, _ = x.shape
    depth = norm_mix.shape[0]
    xt = x.reshape(batch * seq, D_MODEL)
    ltri = (lax.broadcasted_iota(jnp.int32, (TM_PROJ, TM_PROJ), 0)
            > lax.broadcasted_iota(jnp.int32, (TM_PROJ, TM_PROJ), 1)).astype(BF16)
    lane_group = jnp.arange(MIX_W) // HEAD_DIM
    gmat = (lane_group[:, None] == lane_group[None, :]).astype(BF16)
    key = jnp.arange(CHUNK)
    cmat = jnp.concatenate([(key[:, None] >= key[None, :]).astype(BF16), jnp.ones((CHUNK, LANES), BF16)], axis=1)
    cmat = jnp.concatenate([cmat, cmat], axis=0)
    for l in range(depth):
        i = l // 2
        gain_mix = norm_mix[l].reshape(1, D_MODEL)
        gain_ffn = norm_ffn[l].reshape(1, D_MODEL)
        wr_hi, wr_lo = _router_weights(router_group[l], router_expert[l])
        if l % 2 == 0:
            uv, q, k, v = _even_inproj(xt, gain_mix, even_w_in[i].astype(BF16))
            bias_full = jnp.repeat(gmlp_b_s[i].T, HEAD_DIM, axis=1)
            mix = _gate_attn(uv, q, k, v, gmlp_w_s[i], bias_full, gmlp_v_gain[i].reshape(1, MIX_W), gmat, cmat,
                             batch, seq)
            x1, xn, rinfo, counts = _even_outproj(xt, mix, even_w_out[i].astype(BF16), gain_ffn,
                                                  wr_hi, wr_lo, ltri)
        else:
            x1, xn, rinfo, counts = _odd_mixer(xt, gain_mix, odd_w_in[i].astype(BF16), odd_conv_w[i],
                                               odd_w_out[i].astype(BF16), gain_ffn, wr_hi, wr_lo, ltri,
                                               batch, seq)
        xt = _moe(x1, xn, rinfo, counts, w_gate, w_up, w_down, l,
                  norm_final if l == depth - 1 else None)
    return xt.reshape(batch, seq, D_MODEL)
```

```python
import functools

import jax
import jax.numpy as jnp
from jax import lax
from jax.experimental import pallas as pl
from jax.experimental.pallas import tpu as pltpu

F32 = jnp.float32
BF16 = jnp.bfloat16

D_MODEL = 1024
N_PAIR = 4
HEAD_DIM = 64
MIX_W = 512
CHUNK = 128
N_GROUPS = 4
EXPERTS_PER_GROUP = 8
N_EXPERTS = 32
EXPERT_FF = 512
EPS = 1e-6
LANES = 128

ROW_TILE = D_MODEL // LANES
RANK_SPAN = 16384
DMA_UNROLL = 8
KEY_UNROLL = 2
ROUTER_LANES = 128
EXPERT_LANE0 = N_GROUPS

TM_PROJ = 512
TM_MOE = 512
TM_DISPATCH = 512
TM_COMBINE = 512
VMEM_LIMIT = 56 * 1024 * 1024


def _rms_norm(x, gain):
    return x * lax.rsqrt(jnp.mean(x * x, axis=-1, keepdims=True) + EPS) * gain


def _split_bf16(x):
    hi = x.astype(BF16)
    lo = (x - hi.astype(F32)).astype(BF16)
    return hi, lo


def _dot(a, b):
    return jnp.dot(a, b, preferred_element_type=F32)


def _store_row_tiles(ref, x):
    rows = x.shape[0]
    for s in range(ROW_TILE):
        ref[pl.ds(s, rows, stride=ROW_TILE), :] = x[:, s * LANES:(s + 1) * LANES]


def _load_row_tiles(ref, rows):
    return jnp.concatenate([ref[pl.ds(s, rows, stride=ROW_TILE), :] for s in range(ROW_TILE)], axis=1)


def _store_row_tiles_bf16(ref, stage_ref, x):
    _store_row_tiles(stage_ref, x)
    ref[...] = stage_ref[...].astype(BF16)


def _load_row_tiles_bf16(ref, stage_ref, rows):
    stage_ref[...] = ref[...].astype(F32)
    return _load_row_tiles(stage_ref, rows)


def _even_inproj_kernel(x_ref, gain_ref, w_ref, uv_ref, q_ref, k_ref, v_ref):
    h = _rms_norm(x_ref[...], gain_ref[...]).astype(BF16)
    uv = _dot(h, w_ref[:, 0:2 * MIX_W])
    uv = 0.5 * uv * (1.0 + lax.erf(uv * (0.5 ** 0.5)))
    uv_ref[...] = uv.astype(BF16)
    q_ref[...] = (_dot(h, w_ref[:, 2 * MIX_W:3 * MIX_W]) * (HEAD_DIM ** -0.5)).astype(BF16)
    k_ref[...] = _dot(h, w_ref[:, 3 * MIX_W:4 * MIX_W]).astype(BF16)
    v_ref[...] = _dot(h, w_ref[:, 4 * MIX_W:5 * MIX_W]).astype(BF16)


def _even_inproj(x, gain, w_in):
    t = x.shape[0]
    row = lambda w: pl.BlockSpec((TM_PROJ, w), lambda i: (i, 0))
    return pl.pallas_call(
        _even_inproj_kernel,
        grid=(t // TM_PROJ,),
        in_specs=[row(D_MODEL),
                  pl.BlockSpec((1, D_MODEL), lambda i: (0, 0)),
                  pl.BlockSpec((D_MODEL, 5 * MIX_W), lambda i: (0, 0))],
        out_specs=[row(2 * MIX_W), row(MIX_W), row(MIX_W), row(MIX_W)],
        out_shape=[jax.ShapeDtypeStruct((t, 2 * MIX_W), BF16)] + [jax.ShapeDtypeStruct((t, MIX_W), BF16)] * 3,
        compiler_params=pltpu.CompilerParams(dimension_semantics=("arbitrary",), vmem_limit_bytes=VMEM_LIMIT),
        name="even_inproj",
    )(x, gain, w_in)


def _gate_attn_kernel(uv_ref, q_ref, k_ref, v_ref, ws_ref, bias_ref, vgain_ref, gmat_ref, cmat2_ref, out_ref,
                      kh_ref, vh_ref, carry_ref, acc_ref):
    i = pl.program_id(1)
    row = lax.broadcasted_iota(jnp.int32, (CHUNK, CHUNK), 0)
    col = lax.broadcasted_iota(jnp.int32, (CHUNK, CHUNK), 1)
    first_half = col < HEAD_DIM

    u = uv_ref[:, 0:MIX_W].astype(F32)
    v = uv_ref[:, MIX_W:2 * MIX_W].astype(F32)
    sq_hi, sq_lo = _split_bf16(v * v)
    mean_sq = (_dot(sq_hi, gmat_ref[...]) + _dot(sq_lo, gmat_ref[...])) * (1.0 / HEAD_DIM)
    vn = (v * lax.rsqrt(mean_sq + EPS) * vgain_ref[...]).astype(BF16)
    tril = row >= col
    for p in range(N_PAIR):
        cols = slice(p * LANES, (p + 1) * LANES)
        vp = vn[:, cols]
        m0 = _dot(jnp.where(tril, ws_ref[2 * p], 0.0).astype(BF16), vp)
        m1 = _dot(jnp.where(tril, ws_ref[2 * p + 1], 0.0).astype(BF16), vp)
        mixed = jnp.where(first_half, m0, m1) + bias_ref[:, cols]
        out_ref[:, cols] = (u[:, cols] * mixed).astype(out_ref.dtype)

    row2 = lax.broadcasted_iota(jnp.int32, (CHUNK, 2 * CHUNK), 0)
    col2 = lax.broadcasted_iota(jnp.int32, (CHUNK, 2 * CHUNK), 1)
    strict2 = (col2 & (CHUNK - 1)) < row2
    pairs = range(N_PAIR)
    pcols = [slice(p * LANES, (p + 1) * LANES) for p in pairs]

    @pl.when(i == 0)
    def _():
        lane = lax.broadcasted_iota(jnp.int32, (CHUNK, MIX_W), 1)
        head0 = (lane & HEAD_DIM) == 0

        def fill(blk, c):
            rows = pl.ds(pl.multiple_of(blk * CHUNK, CHUNK), CHUNK)
            for src, dst in ((k_ref, kh_ref), (v_ref, vh_ref)):
                x = src[rows, :]
                zero = jnp.zeros_like(x)
                dst[0, rows, :] = jnp.where(head0, x, zero)
                dst[1, rows, :] = jnp.where(head0, zero, x)
            return c

        lax.fori_loop(0, k_ref.shape[0] // CHUNK, fill, 0)

    def per_head_rows(ref, rows, cols):
        return jnp.concatenate([ref[0, rows, cols], ref[1, rows, cols]], axis=0)

    def key_blocks(js, first):
        assert len(js) == 1 or not first
        rows = [pl.ds(pl.multiple_of(j * CHUNK, CHUNK), CHUNK) for j in js]
        chains = [(b, p) for b in range(len(js)) for p in pairs]
        z = {(b, p): lax.dot_general(q_ref[:, pcols[p]], per_head_rows(kh_ref, rows[b], pcols[p]),
                                     (((1,), (1,)), ((), ())), preferred_element_type=F32)
             for b, p in chains}
        suffix, total = {}, {}
        for c in chains:
            sp = jnp.maximum(z[c], 0.0) + jnp.log(1.0 + jnp.exp(-jnp.abs(z[c])))
            if first:
                sp = jnp.where(strict2, sp, 0.0)
            hi, lo = _split_bf16(sp)
            lhs = jnp.concatenate([jnp.concatenate([hi[:, 0:CHUNK], lo[:, 0:CHUNK]], axis=1),
                                   jnp.concatenate([hi[:, CHUNK:], lo[:, CHUNK:]], axis=1)], axis=0)
            s = _dot(lhs, cmat2_ref[...])
            suffix[c] = jnp.concatenate([s[0:CHUNK, 0:CHUNK], s[CHUNK:, 0:CHUNK]], axis=1)
            total[c] = jnp.concatenate([s[0:CHUNK, CHUNK:], s[CHUNK:, CHUNK:]], axis=1)
        pv = {}
        for b, p in chains:
            exponent = z[b, p] - suffix[b, p]
            if not first:
                carry = carry_ref[p]
                for earlier in range(b):
                    carry = carry + total[earlier, p]
                exponent = exponent - carry
            a = jnp.exp(exponent)
            if first:
                a = jnp.where(strict2, a, 0.0)
            pv[b, p] = _dot(a.astype(BF16), per_head_rows(vh_ref, rows[b], pcols[p]))
        for p in pairs:
            pv_sum, total_sum = pv[0, p], total[0, p]
            for b in range(1, len(js)):
                pv_sum, total_sum = pv_sum + pv[b, p], total_sum + total[b, p]
            if first:
                acc_ref[p] = pv_sum
                carry_ref[p] = total_sum
            else:
                acc_ref[p] += pv_sum
                carry_ref[p] += total_sum

    key_blocks([i], True)

    def body(step, c):
        j = i - 1 - KEY_UNROLL * step
        key_blocks([j - u for u in range(KEY_UNROLL)], False)
        return c

    lax.fori_loop(0, i // KEY_UNROLL, body, 0)

    def tail(step, c):
        key_blocks([i % KEY_UNROLL - 1 - step], False)
        return c

    lax.fori_loop(0, i % KEY_UNROLL, tail, 0)

    for p in pairs:
        out_ref[:, MIX_W + p * LANES:MIX_W + (p + 1) * LANES] = acc_ref[p].astype(out_ref.dtype)


def _gate_attn(uv, q, k, v, w_s, bias_full, v_gain, gmat, cmat, batch, seq):
    nq = seq // CHUNK
    t = uv.shape[0]
    qrow = lambda w: pl.BlockSpec((CHUNK, w), lambda b, i: (b * nq + i, 0))
    kv = pl.BlockSpec((seq, MIX_W), lambda b, i: (b, 0))
    full = lambda shape: pl.BlockSpec(shape, lambda b, i: (0,) * len(shape))
    return pl.pallas_call(
        _gate_attn_kernel,
        grid=(batch, nq),
        in_specs=[qrow(2 * MIX_W), qrow(MIX_W), kv, kv,
                  full((2 * N_PAIR, CHUNK, CHUNK)), full((CHUNK, MIX_W)), full((1, MIX_W)),
                  full((MIX_W, MIX_W)), full((2 * CHUNK, 2 * LANES))],
        out_specs=qrow(2 * MIX_W),
        out_shape=jax.ShapeDtypeStruct((t, 2 * MIX_W), BF16),
        scratch_shapes=[pltpu.VMEM((2, seq, MIX_W), BF16), pltpu.VMEM((2, seq, MIX_W), BF16),
                        pltpu.VMEM((N_PAIR, CHUNK, 2 * LANES), F32),
                        pltpu.VMEM((N_PAIR, CHUNK, LANES), F32)],
        compiler_params=pltpu.CompilerParams(dimension_semantics=("arbitrary", "arbitrary"),
                                             vmem_limit_bytes=VMEM_LIMIT),
        name="gate_attn",
    )(uv, q, k, v, w_s, bias_full, v_gain, gmat, cmat)


def _router_epilogue(x1, gain_ref, wr_hi_ref, wr_lo_ref, ltri_ref, base_ref, stage_ref, is_first,
                     xn_ref, rinfo_ref, counts_ref):
    tm = x1.shape[0]
    xn = _rms_norm(x1, gain_ref[...])
    _store_row_tiles_bf16(xn_ref, stage_ref, xn)
    x_hi, x_lo = _split_bf16(xn)
    logits = _dot(x_hi, wr_hi_ref[...]) + _dot(x_lo, wr_hi_ref[...]) + _dot(x_hi, wr_lo_ref[...])

    lane = lax.broadcasted_iota(jnp.int32, (tm, ROUTER_LANES), 1)
    neg_inf = jnp.float32(-jnp.inf)
    lane_f = lane.astype(F32)
    big = jnp.float32(ROUTER_LANES)
    is_group = lane < N_GROUPS
    lg = jnp.where(is_group, logits, neg_inf)
    gmax = jnp.max(lg, axis=1, keepdims=True)
    gsum = jnp.sum(jnp.where(is_group, jnp.exp(logits - gmax), 0.0), axis=1, keepdims=True)
    p_top = 1.0 / gsum
    top_g = jnp.min(jnp.where(lg == gmax, lane_f, big), axis=1, keepdims=True)

    expert = lane - EXPERT_LANE0
    expert_group = (expert >> 3).astype(F32)
    in_group = (expert >= 0) & (expert < N_EXPERTS) & (expert_group == top_g)
    le = jnp.where(in_group, logits, neg_inf)
    m1 = jnp.max(le, axis=1, keepdims=True)
    i1 = jnp.min(jnp.where(le == m1, lane_f, big), axis=1, keepdims=True)
    le2 = jnp.where(lane_f == i1, neg_inf, le)
    m2 = jnp.max(le2, axis=1, keepdims=True)
    i2 = jnp.min(jnp.where(le2 == m2, lane_f, big), axis=1, keepdims=True)
    r = jnp.exp(m2 - m1)
    w1 = p_top / (1.0 + r)
    w2 = p_top * r / (1.0 + r)

    @pl.when(is_first)
    def _():
        base_ref[...] = jnp.zeros_like(base_ref)

    hit1 = lane_f == i1
    hit2 = lane_f == i2
    used = (hit1 | hit2).astype(F32)
    before = _dot(ltri_ref[...], used.astype(BF16)) + base_ref[0:1, :]
    rank1 = jnp.sum(jnp.where(hit1, before, 0.0), axis=1, keepdims=True)
    rank2 = jnp.sum(jnp.where(hit2, before, 0.0), axis=1, keepdims=True)
    new_base = base_ref[0:1, :] + jnp.sum(used, axis=0, keepdims=True)
    base_ref[...] = jnp.broadcast_to(new_base, base_ref.shape)
    counts_ref[...] = jnp.broadcast_to(new_base, counts_ref.shape)

    code1 = (i1 - EXPERT_LANE0) * RANK_SPAN + rank1
    code2 = (i2 - EXPERT_LANE0) * RANK_SPAN + rank2
    info = jnp.zeros((tm, ROUTER_LANES), F32)
    for idx, val in enumerate((code1, code2, w1, w2)):
        info = jnp.where(lane == idx, val, info)
    rinfo_ref[...] = info


def _router_specs(tm, index):
    const = lambda shape: pl.BlockSpec(shape, lambda *g: (0,) * len(shape))
    in_specs = [const((1, D_MODEL)), const((D_MODEL, ROUTER_LANES)), const((D_MODEL, ROUTER_LANES)),
                const((tm, tm))]
    out_specs = [pl.BlockSpec((tm * ROW_TILE, LANES), index), pl.BlockSpec((tm, ROUTER_LANES), index),
                 const((8, ROUTER_LANES))]
    return in_specs, out_specs


def _router_out_shapes(t):
    return [jax.ShapeDtypeStruct((t * ROW_TILE, LANES), BF16), jax.ShapeDtypeStruct((t, ROUTER_LANES), F32),
            jax.ShapeDtypeStruct((8, ROUTER_LANES), F32)]


def _even_outproj_kernel(x_ref, mix_ref, wout_ref, gain_ref, wr_hi_ref, wr_lo_ref, ltri_ref,
                         x1_ref, xn_ref, rinfo_ref, counts_ref, base_ref, stage_ref):
    x1 = x_ref[...] + _dot(mix_ref[...], wout_ref[...])
    x1_ref[...] = x1
    _router_epilogue(x1, gain_ref, wr_hi_ref, wr_lo_ref, ltri_ref, base_ref, stage_ref, pl.program_id(0) == 0,
                     xn_ref, rinfo_ref, counts_ref)


def _even_outproj(x, mix, w_out, gain, wr_hi, wr_lo, ltri):
    t = x.shape[0]
    index = lambda i: (i, 0)
    r_in, r_out = _router_specs(TM_PROJ, index)
    return pl.pallas_call(
        _even_outproj_kernel,
        grid=(t // TM_PROJ,),
        in_specs=[pl.BlockSpec((TM_PROJ, D_MODEL), index), pl.BlockSpec((TM_PROJ, D_MODEL), index),
                  pl.BlockSpec((D_MODEL, D_MODEL), lambda i: (0, 0))] + r_in,
        out_specs=[pl.BlockSpec((TM_PROJ, D_MODEL), index)] + r_out,
        out_shape=[jax.ShapeDtypeStruct((t, D_MODEL), F32)] + _router_out_shapes(t),
        scratch_shapes=[pltpu.VMEM((8, ROUTER_LANES), F32), pltpu.VMEM((TM_PROJ * ROW_TILE, LANES), F32)],
        compiler_params=pltpu.CompilerParams(dimension_semantics=("arbitrary",), vmem_limit_bytes=VMEM_LIMIT),
        name="even_outproj_router",
    )(x, mix, w_out, gain, wr_hi, wr_lo, ltri)


CONV_PAD = 8


def _odd_mixer_kernel(x_ref, gain_mix_ref, win_ref, convw_ref, wout_ref, gain_ref, wr_hi_ref, wr_lo_ref,
                      ltri_ref, x1_ref, xn_ref, rinfo_ref, counts_ref, base_ref, ybuf_ref, stage_ref):
    s = pl.program_id(1)
    tm = x_ref.shape[0]
    x = x_ref[...]
    h = _rms_norm(x, gain_mix_ref[...]).astype(BF16)
    gate_b = _dot(h, win_ref[:, 0:D_MODEL])
    gate_c = _dot(h, win_ref[:, D_MODEL:2 * D_MODEL])
    y = gate_c * _dot(h, win_ref[:, 2 * D_MODEL:3 * D_MODEL])

    @pl.when(s == 0)
    def _():
        ybuf_ref[0:CONV_PAD, :] = jnp.zeros((CONV_PAD, D_MODEL), F32)

    ybuf_ref[CONV_PAD:CONV_PAD + tm, :] = y
    y1 = ybuf_ref[CONV_PAD - 1:CONV_PAD - 1 + tm, :]
    y2 = ybuf_ref[CONV_PAD - 2:CONV_PAD - 2 + tm, :]
    conv = convw_ref[0:1, :] * y2 + convw_ref[1:2, :] * y1 + convw_ref[2:3, :] * y
    ybuf_ref[0:CONV_PAD, :] = y[tm - CONV_PAD:tm, :]
    x1 = x + _dot((gate_b * conv).astype(BF16), wout_ref[...])
    x1_ref[...] = x1
    _router_epilogue(x1, gain_ref, wr_hi_ref, wr_lo_ref, ltri_ref, base_ref, stage_ref,
                     (pl.program_id(0) == 0) & (s == 0), xn_ref, rinfo_ref, counts_ref)


def _odd_mixer(x, gain_mix, w_in, conv_w, w_out, gain, wr_hi, wr_lo, ltri, batch, seq):
    t = x.shape[0]
    ns = seq // TM_PROJ
    index = lambda b, s: (b * ns + s, 0)
    const = lambda shape: pl.BlockSpec(shape, lambda b, s: (0,) * len(shape))
    r_in, r_out = _router_specs(TM_PROJ, index)
    return pl.pallas_call(
        _odd_mixer_kernel,
        grid=(batch, ns),
        in_specs=[pl.BlockSpec((TM_PROJ, D_MODEL), index), const((1, D_MODEL)),
                  const((D_MODEL, 3 * D_MODEL)), const((3, D_MODEL)), const((D_MODEL, D_MODEL))] + r_in,
        out_specs=[pl.BlockSpec((TM_PROJ, D_MODEL), index)] + r_out,
        out_shape=[jax.ShapeDtypeStruct((t, D_MODEL), F32)] + _router_out_shapes(t),
        scratch_shapes=[pltpu.VMEM((8, ROUTER_LANES), F32),
                        pltpu.VMEM((CONV_PAD + TM_PROJ, D_MODEL), F32),
                        pltpu.VMEM((TM_PROJ * ROW_TILE, LANES), F32)],
        compiler_params=pltpu.CompilerParams(dimension_semantics=("arbitrary", "arbitrary"),
                                             vmem_limit_bytes=VMEM_LIMIT),
        name="odd_mixer_router",
    )(x, gain_mix, w_in, conv_w, w_out, gain, wr_hi, wr_lo, ltri)


def _row_tile(ref, first_row):
    return ref.at[pl.ds(pl.multiple_of(first_row, ROW_TILE), ROW_TILE), :]


def _for_each_row_dma(tm, copy):
    def start(blk, c):
        for u in range(DMA_UNROLL):
            for slot in range(2):
                copy(blk * DMA_UNROLL + u, slot).start(priority=slot)
        return c

    def wait(blk, c):
        for u in range(DMA_UNROLL):
            for slot in range(2):
                copy(blk * DMA_UNROLL + u, slot).wait()
        return c

    lax.fori_loop(0, tm // DMA_UNROLL, start, 0)
    lax.fori_loop(0, tm // DMA_UNROLL, wait, 0)


def _sorted_first_row(row_start_ref, code):
    return (row_start_ref[code >> 14] + (code & (RANK_SPAN - 1))) * ROW_TILE


def _dispatch_kernel(row_start_ref, last_tile_ref, n_used_ref, code_ref, xn_ref, xs_ref, zeros_ref, sem, zero_sem):
    tm = xn_ref.shape[0] // ROW_TILE
    tile_rows = TM_MOE * ROW_TILE
    n_tiles = xs_ref.shape[0] // tile_rows

    @pl.when(pl.program_id(0) == 0)
    def _():
        zeros_ref[...] = jnp.zeros_like(zeros_ref)

        def zero_copy(tile):
            start = pl.multiple_of(tile * tile_rows, tile_rows)
            return pltpu.make_async_copy(zeros_ref, xs_ref.at[pl.ds(start, tile_rows), :], zero_sem)

        def for_each_zero_tile(fn):
            def last(e, c):
                @pl.when(last_tile_ref[e] >= 0)
                def _():
                    fn(zero_copy(last_tile_ref[e]))
                return c

            def unused(tile, c):
                fn(zero_copy(tile))
                return c

            lax.fori_loop(0, N_EXPERTS, last, 0)
            lax.fori_loop(n_used_ref[0], n_tiles, unused, 0)

        for_each_zero_tile(lambda dma: dma.start())
        for_each_zero_tile(lambda dma: dma.wait())

    def copy(r, slot):
        return pltpu.make_async_copy(_row_tile(xn_ref, r * ROW_TILE),
                                     _row_tile(xs_ref, _sorted_first_row(row_start_ref, code_ref[2 * r + slot])), sem)

    _for_each_row_dma(tm, copy)


def _dispatch(row_start, last_tile, n_used, codes, xn_rows, n_tiles):
    t = xn_rows.shape[0] // ROW_TILE
    grid_spec = pltpu.PrefetchScalarGridSpec(
        num_scalar_prefetch=3,
        grid=(t // TM_DISPATCH,),
        in_specs=[pl.BlockSpec((2 * TM_DISPATCH,), lambda i, rs, lt, nu: (i,), memory_space=pltpu.SMEM),
                  pl.BlockSpec((TM_DISPATCH * ROW_TILE, LANES), lambda i, rs, lt, nu: (i, 0))],
        out_specs=pl.BlockSpec(memory_space=pl.ANY),
        scratch_shapes=[pltpu.VMEM((TM_MOE * ROW_TILE, LANES), BF16),
                        pltpu.SemaphoreType.DMA(()), pltpu.SemaphoreType.DMA(())],
    )
    return pl.pallas_call(
        _dispatch_kernel,
        grid_spec=grid_spec,
        out_shape=jax.ShapeDtypeStruct((n_tiles * TM_MOE * ROW_TILE, LANES), BF16),
        compiler_params=pltpu.CompilerParams(dimension_semantics=("arbitrary",)),
        name="moe_dispatch",
    )(row_start, last_tile, n_used, codes, xn_rows)


def _expert_ffn_kernel(tile_start_ref, n_used_ref, xs_ref, wg_ref, wu_ref, wd_ref, ys_ref,
                       xbuf_ref, ybuf_ref, xstage_ref, ystage_ref, wg_bf_ref, wu_bf_ref, wd_bf_ref, in_sem, out_sem):
    e = pl.program_id(0)
    n_used = n_used_ref[0]
    first, last = tile_start_ref[e], tile_start_ref[e + 1]
    tile_rows = TM_MOE * ROW_TILE
    n_tiles = ys_ref.shape[0] // tile_rows

    def rows_of(tile):
        return pl.ds(pl.multiple_of(tile * tile_rows, tile_rows), tile_rows)

    def xs_copy(tile, slot):
        return pltpu.make_async_copy(xs_ref.at[rows_of(tile), :], xbuf_ref.at[slot], in_sem.at[slot])

    def ys_copy(tile, slot):
        return pltpu.make_async_copy(ybuf_ref.at[slot], ys_ref.at[rows_of(tile), :], out_sem.at[slot])

    @pl.when(e == 0)
    def _():
        xs_copy(0, 0).start()

    @pl.when(last > first)
    def _():
        wg_bf_ref[...] = wg_ref[...].astype(BF16)
        wu_bf_ref[...] = wu_ref[...].astype(BF16)
        wd_bf_ref[...] = wd_ref[...].astype(BF16)

    def tile_body(tile, c):
        slot = tile & 1
        xs_copy(tile, slot).wait()

        @pl.when(tile + 1 < n_used)
        def _():
            xs_copy(tile + 1, 1 - slot).start()

        x = _load_row_tiles_bf16(xbuf_ref.at[slot], xstage_ref, TM_MOE).astype(BF16)
        g = _dot(x, wg_bf_ref[...])
        up = _dot(x, wu_bf_ref[...])
        h = (g * jax.nn.sigmoid(g)) * up
        y = _dot(h.astype(BF16), wd_bf_ref[...])

        @pl.when(tile >= 2)
        def _():
            ys_copy(tile - 2, slot).wait()

        _store_row_tiles_bf16(ybuf_ref.at[slot], ystage_ref, y)
        ys_copy(tile, slot).start()
        return c

    lax.fori_loop(first, last, tile_body, 0)

    @pl.when(e == pl.num_programs(0) - 1)
    def _():
        @pl.when(n_used >= 2)
        def _():
            ys_copy(n_used - 2, n_used & 1).wait()

        ys_copy(n_used - 1, (n_used - 1) & 1).wait()
        ybuf_ref[0] = jnp.zeros(ybuf_ref.shape[1:], ybuf_ref.dtype)

        def start(tile, c):
            ys_copy(tile, 0).start()
            return c

        def wait(tile, c):
            ys_copy(tile, 0).wait()
            return c

        lax.fori_loop(n_used, n_tiles, start, 0)
        lax.fori_loop(n_used, n_tiles, wait, 0)


def _expert_ffn(tile_start, n_used, xs, w_gate, w_up, w_down, layer):
    tile_rows = TM_MOE * ROW_TILE
    w_index = lambda e, ts, nu: (layer, e, 0, 0)
    grid_spec = pltpu.PrefetchScalarGridSpec(
        num_scalar_prefetch=2,
        grid=(N_EXPERTS,),
        in_specs=[pl.BlockSpec(memory_space=pl.ANY),
                  pl.BlockSpec((None, None, D_MODEL, EXPERT_FF), w_index),
                  pl.BlockSpec((None, None, D_MODEL, EXPERT_FF), w_index),
                  pl.BlockSpec((None, None, EXPERT_FF, D_MODEL), w_index)],
        out_specs=pl.BlockSpec(memory_space=pl.ANY),
        scratch_shapes=[pltpu.VMEM((2, tile_rows, LANES), BF16), pltpu.VMEM((2, tile_rows, LANES), BF16),
                        pltpu.VMEM((tile_rows, LANES), F32), pltpu.VMEM((tile_rows, LANES), F32),
                        pltpu.VMEM((D_MODEL, EXPERT_FF), BF16), pltpu.VMEM((D_MODEL, EXPERT_FF), BF16),
                        pltpu.VMEM((EXPERT_FF, D_MODEL), BF16),
                        pltpu.SemaphoreType.DMA((2,)), pltpu.SemaphoreType.DMA((2,))],
    )
    return pl.pallas_call(
        _expert_ffn_kernel,
        grid_spec=grid_spec,
        out_shape=jax.ShapeDtypeStruct(xs.shape, BF16),
        compiler_params=pltpu.CompilerParams(dimension_semantics=("arbitrary",), vmem_limit_bytes=VMEM_LIMIT),
        name="expert_ffn",
    )(tile_start, n_used, xs, w_gate, w_up, w_down)


def _combine_kernel(row_start_ref, code_ref, x_ref, rinfo_ref, gain_ref, ys_ref, out_ref, buf0_ref, buf1_ref,
                    stage0_ref, stage1_ref, sem, *, final_norm):
    tm = x_ref.shape[0]
    bufs = (buf0_ref, buf1_ref)

    def copy(r, slot):
        return pltpu.make_async_copy(_row_tile(ys_ref, _sorted_first_row(row_start_ref, code_ref[2 * r + slot])),
                                     _row_tile(bufs[slot], r * ROW_TILE), sem)

    _for_each_row_dma(tm, copy)
    out = (x_ref[...] + rinfo_ref[:, 2:3] * _load_row_tiles_bf16(buf0_ref, stage0_ref, tm)
           + rinfo_ref[:, 3:4] * _load_row_tiles_bf16(buf1_ref, stage1_ref, tm))
    if final_norm:
        out = _rms_norm(out, gain_ref[...])
    out_ref[...] = out


def _combine(row_start, codes, x1, rinfo, gain, ys, final_norm):
    t = x1.shape[0]
    grid_spec = pltpu.PrefetchScalarGridSpec(
        num_scalar_prefetch=1,
        grid=(t // TM_COMBINE,),
        in_specs=[pl.BlockSpec((2 * TM_COMBINE,), lambda i, rs: (i,), memory_space=pltpu.SMEM),
                  pl.BlockSpec((TM_COMBINE, D_MODEL), lambda i, rs: (i, 0)),
                  pl.BlockSpec((TM_COMBINE, ROUTER_LANES), lambda i, rs: (i, 0)),
                  pl.BlockSpec((1, D_MODEL), lambda i, rs: (0, 0)),
                  pl.BlockSpec(memory_space=pl.ANY)],
        out_specs=pl.BlockSpec((TM_COMBINE, D_MODEL), lambda i, rs: (i, 0)),
        scratch_shapes=[pltpu.VMEM((TM_COMBINE * ROW_TILE, LANES), BF16),
                        pltpu.VMEM((TM_COMBINE * ROW_TILE, LANES), BF16),
                        pltpu.VMEM((TM_COMBINE * ROW_TILE, LANES), F32),
                        pltpu.VMEM((TM_COMBINE * ROW_TILE, LANES), F32),
                        pltpu.SemaphoreType.DMA(())],
    )
    return pl.pallas_call(
        functools.partial(_combine_kernel, final_norm=final_norm),
        grid_spec=grid_spec,
        out_shape=jax.ShapeDtypeStruct((t, D_MODEL), F32),
        compiler_params=pltpu.CompilerParams(dimension_semantics=("arbitrary",), vmem_limit_bytes=VMEM_LIMIT),
        name="moe_combine",
    )(row_start, codes, x1, rinfo, gain, ys)


def _moe(x1, xn, rinfo, counts, w_gate, w_up, w_down, layer, final_gain):
    t = x1.shape[0]
    n_tiles = (2 * t) // TM_MOE + N_EXPERTS
    cnt = counts[0, EXPERT_LANE0:EXPERT_LANE0 + N_EXPERTS].astype(jnp.int32)
    tiles_per_expert = (cnt + TM_MOE - 1) // TM_MOE
    tile_end = jnp.cumsum(tiles_per_expert)
    row_start = (tile_end - tiles_per_expert) * TM_MOE
    n_used = tile_end[-1:]
    codes = rinfo[:, 0:2].astype(jnp.int32).reshape(-1)
    tile_start = jnp.concatenate([tile_end - tiles_per_expert, n_used]).astype(jnp.int32)

    last_tile = jnp.where(tiles_per_expert > 0, tile_end - 1, -1).astype(jnp.int32)
    n_used = n_used.astype(jnp.int32)
    xs = _dispatch(row_start, last_tile, n_used, codes, xn, n_tiles)
    ys = _expert_ffn(tile_start, n_used, xs, w_gate, w_up, w_down, layer)
    gain = jnp.ones((1, D_MODEL), F32) if final_gain is None else final_gain.reshape(1, D_MODEL)
    return _combine(row_start, codes, x1, rinfo, gain, ys, final_gain is not None)


def _router_weights(router_group, router_expert):
    w = jnp.concatenate([router_group, jnp.transpose(router_expert, (1, 0, 2)).reshape(D_MODEL, N_EXPERTS)],
                        axis=1)
    w = jnp.pad(w, ((0, 0), (0, ROUTER_LANES - w.shape[1])))
    return _split_bf16(w)


def kernel(x, even_w_in, even_w_out, gmlp_w_s, gmlp_b_s, gmlp_v_gain, odd_w_in, odd_conv_w, odd_w_out,
           norm_mix, norm_ffn, router_group, router_expert, w_gate, w_up, w_down, norm_final):
    batch, seq, _ = x.shape
    depth = norm_mix.shape[0]
    xt = x.reshape(batch * seq, D_MODEL)
    ltri = (lax.broadcasted_iota(jnp.int32, (TM_PROJ, TM_PROJ), 0)
            > lax.broadcasted_iota(jnp.int32, (TM_PROJ, TM_PROJ), 1)).astype(BF16)
    lane_group = jnp.arange(MIX_W) // HEAD_DIM
    gmat = (lane_group[:, None] == lane_group[None, :]).astype(BF16)
    key = jnp.arange(CHUNK)
    cmat = jnp.concatenate([(key[:, None] >= key[None, :]).astype(BF16), jnp.ones((CHUNK, LANES), BF16)], axis=1)
    cmat = jnp.concatenate([cmat, cmat], axis=0)
    for l in range(depth):
        i = l // 2
        gain_mix = norm_mix[l].reshape(1, D_MODEL)
        gain_ffn = norm_ffn[l].reshape(1, D_MODEL)
        wr_hi, wr_lo = _router_weights(router_group[l], router_expert[l])
        if l % 2 == 0:
            uv, q, k, v = _even_inproj(xt, gain_mix, even_w_in[i].astype(BF16))
            bias_full = jnp.repeat(gmlp_b_s[i].T, HEAD_DIM, axis=1)
            mix = _gate_attn(uv, q, k, v, gmlp_w_s[i], bias_full, gmlp_v_gain[i].reshape(1, MIX_W), gmat, cmat,
                             batch, seq)
            x1, xn, rinfo, counts = _even_outproj(xt, mix, even_w_out[i].astype(BF16), gain_ffn,
                                                  wr_hi, wr_lo, ltri)
        else:
            x1, xn, rinfo, counts = _odd_mixer(xt, gain_mix, odd_w_in[i].astype(BF16), odd_conv_w[i],
                                               odd_w_out[i].astype(BF16), gain_ffn, wr_hi, wr_lo, ltri,
                                               batch, seq)
        xt = _moe(x1, xn, rinfo, counts, w_gate, w_up, w_down, l,
                  norm_final if l == depth - 1 else None)
    return xt.reshape(batch, seq, D_MODEL)
```

```python
import functools

import jax
import jax.numpy as jnp
from jax import lax
from jax.experimental import pallas as pl
from jax.experimental.pallas import tpu as pltpu

F32 = jnp.float32
BF16 = jnp.bfloat16

D_MODEL = 1024
N_PAIR = 4
HEAD_DIM = 64
MIX_W = 512
CHUNK = 128
N_GROUPS = 4
EXPERTS_PER_GROUP = 8
N_EXPERTS = 32
EXPERT_FF = 512
EPS = 1e-6
LANES = 128

ROW_TILE = D_MODEL // LANES
RANK_SPAN = 16384
DMA_UNROLL = 8
KEY_UNROLL = 2
ROUTER_LANES = 128
EXPERT_LANE0 = N_GROUPS

TM_PROJ = 512
TM_MOE = 256
TM_DISPATCH = 512
TM_COMBINE = 512
VMEM_LIMIT = 56 * 1024 * 1024


def _rms_norm(x, gain):
    return x * lax.rsqrt(jnp.mean(x * x, axis=-1, keepdims=True) + EPS) * gain


def _split_bf16(x):
    hi = x.astype(BF16)
    lo = (x - hi.astype(F32)).astype(BF16)
    return hi, lo


def _dot(a, b):
    return jnp.dot(a, b, preferred_element_type=F32)


def _store_row_tiles(ref, x):
    rows = x.shape[0]
    for s in range(ROW_TILE):
        ref[pl.ds(s, rows, stride=ROW_TILE), :] = x[:, s * LANES:(s + 1) * LANES]


def _load_row_tiles(ref, rows):
    return jnp.concatenate([ref[pl.ds(s, rows, stride=ROW_TILE), :] for s in range(ROW_TILE)], axis=1)


def _even_inproj_kernel(x_ref, gain_ref, w_ref, uv_ref, q_ref, k_ref, v_ref):
    h = _rms_norm(x_ref[...], gain_ref[...]).astype(BF16)
    uv = _dot(h, w_ref[:, 0:2 * MIX_W])
    uv = 0.5 * uv * (1.0 + lax.erf(uv * (0.5 ** 0.5)))
    uv_ref[...] = uv.astype(BF16)
    q_ref[...] = (_dot(h, w_ref[:, 2 * MIX_W:3 * MIX_W]) * (HEAD_DIM ** -0.5)).astype(BF16)
    k_ref[...] = _dot(h, w_ref[:, 3 * MIX_W:4 * MIX_W]).astype(BF16)
    v_ref[...] = _dot(h, w_ref[:, 4 * MIX_W:5 * MIX_W]).astype(BF16)


def _even_inproj(x, gain, w_in):
    t = x.shape[0]
    row = lambda w: pl.BlockSpec((TM_PROJ, w), lambda i: (i, 0))
    return pl.pallas_call(
        _even_inproj_kernel,
        grid=(t // TM_PROJ,),
        in_specs=[row(D_MODEL),
                  pl.BlockSpec((1, D_MODEL), lambda i: (0, 0)),
                  pl.BlockSpec((D_MODEL, 5 * MIX_W), lambda i: (0, 0))],
        out_specs=[row(2 * MIX_W), row(MIX_W), row(MIX_W), row(MIX_W)],
        out_shape=[jax.ShapeDtypeStruct((t, 2 * MIX_W), BF16)] + [jax.ShapeDtypeStruct((t, MIX_W), BF16)] * 3,
        compiler_params=pltpu.CompilerParams(dimension_semantics=("arbitrary",), vmem_limit_bytes=VMEM_LIMIT),
        name="even_inproj",
    )(x, gain, w_in)


def _gate_attn_kernel(uv_ref, q_ref, k_ref, v_ref, ws_ref, bias_ref, vgain_ref, gmat_ref, cmat2_ref, out_ref,
                      kh_ref, vh_ref, carry_ref, acc_ref):
    i = pl.program_id(1)
    row = lax.broadcasted_iota(jnp.int32, (CHUNK, CHUNK), 0)
    col = lax.broadcasted_iota(jnp.int32, (CHUNK, CHUNK), 1)
    first_half = col < HEAD_DIM

    u = uv_ref[:, 0:MIX_W].astype(F32)
    v = uv_ref[:, MIX_W:2 * MIX_W].astype(F32)
    sq_hi, sq_lo = _split_bf16(v * v)
    mean_sq = (_dot(sq_hi, gmat_ref[...]) + _dot(sq_lo, gmat_ref[...])) * (1.0 / HEAD_DIM)
    vn = (v * lax.rsqrt(mean_sq + EPS) * vgain_ref[...]).astype(BF16)
    tril = row >= col
    for p in range(N_PAIR):
        cols = slice(p * LANES, (p + 1) * LANES)
        vp = vn[:, cols]
        m0 = _dot(jnp.where(tril, ws_ref[2 * p], 0.0).astype(BF16), vp)
        m1 = _dot(jnp.where(tril, ws_ref[2 * p + 1], 0.0).astype(BF16), vp)
        mixed = jnp.where(first_half, m0, m1) + bias_ref[:, cols]
        out_ref[:, cols] = (u[:, cols] * mixed).astype(out_ref.dtype)

    row2 = lax.broadcasted_iota(jnp.int32, (CHUNK, 2 * CHUNK), 0)
    col2 = lax.broadcasted_iota(jnp.int32, (CHUNK, 2 * CHUNK), 1)
    strict2 = (col2 & (CHUNK - 1)) < row2
    pairs = range(N_PAIR)
    pcols = [slice(p * LANES, (p + 1) * LANES) for p in pairs]

    @pl.when(i == 0)
    def _():
        lane = lax.broadcasted_iota(jnp.int32, (CHUNK, MIX_W), 1)
        head0 = (lane & HEAD_DIM) == 0

        def fill(blk, c):
            rows = pl.ds(pl.multiple_of(blk * CHUNK, CHUNK), CHUNK)
            for src, dst in ((k_ref, kh_ref), (v_ref, vh_ref)):
                x = src[rows, :]
                zero = jnp.zeros_like(x)
                dst[0, rows, :] = jnp.where(head0, x, zero)
                dst[1, rows, :] = jnp.where(head0, zero, x)
            return c

        lax.fori_loop(0, k_ref.shape[0] // CHUNK, fill, 0)

    def per_head_rows(ref, rows, cols):
        return jnp.concatenate([ref[0, rows, cols], ref[1, rows, cols]], axis=0)

    def key_blocks(js, first):
        assert len(js) == 1 or not first
        rows = [pl.ds(pl.multiple_of(j * CHUNK, CHUNK), CHUNK) for j in js]
        chains = [(b, p) for b in range(len(js)) for p in pairs]
        z = {(b, p): lax.dot_general(q_ref[:, pcols[p]], per_head_rows(kh_ref, rows[b], pcols[p]),
                                     (((1,), (1,)), ((), ())), preferred_element_type=F32)
             for b, p in chains}
        suffix, total = {}, {}
        for c in chains:
            sp = jnp.maximum(z[c], 0.0) + jnp.log(1.0 + jnp.exp(-jnp.abs(z[c])))
            if first:
                sp = jnp.where(strict2, sp, 0.0)
            hi, lo = _split_bf16(sp)
            lhs = jnp.concatenate([jnp.concatenate([hi[:, 0:CHUNK], lo[:, 0:CHUNK]], axis=1),
                                   jnp.concatenate([hi[:, CHUNK:], lo[:, CHUNK:]], axis=1)], axis=0)
            s = _dot(lhs, cmat2_ref[...])
            suffix[c] = jnp.concatenate([s[0:CHUNK, 0:CHUNK], s[CHUNK:, 0:CHUNK]], axis=1)
            total[c] = jnp.concatenate([s[0:CHUNK, CHUNK:], s[CHUNK:, CHUNK:]], axis=1)
        pv = {}
        for b, p in chains:
            exponent = z[b, p] - suffix[b, p]
            if not first:
                carry = carry_ref[p]
                for earlier in range(b):
                    carry = carry + total[earlier, p]
                exponent = exponent - carry
            a = jnp.exp(exponent)
            if first:
                a = jnp.where(strict2, a, 0.0)
            pv[b, p] = _dot(a.astype(BF16), per_head_rows(vh_ref, rows[b], pcols[p]))
        for p in pairs:
            pv_sum, total_sum = pv[0, p], total[0, p]
            for b in range(1, len(js)):
                pv_sum, total_sum = pv_sum + pv[b, p], total_sum + total[b, p]
            if first:
                acc_ref[p] = pv_sum
                carry_ref[p] = total_sum
            else:
                acc_ref[p] += pv_sum
                carry_ref[p] += total_sum

    key_blocks([i], True)

    def body(step, c):
        j = i - 1 - KEY_UNROLL * step
        key_blocks([j - u for u in range(KEY_UNROLL)], False)
        return c

    lax.fori_loop(0, i // KEY_UNROLL, body, 0)

    def tail(step, c):
        key_blocks([i % KEY_UNROLL - 1 - step], False)
        return c

    lax.fori_loop(0, i % KEY_UNROLL, tail, 0)

    for p in pairs:
        out_ref[:, MIX_W + p * LANES:MIX_W + (p + 1) * LANES] = acc_ref[p].astype(out_ref.dtype)


def _gate_attn(uv, q, k, v, w_s, bias_full, v_gain, gmat, cmat, batch, seq):
    nq = seq // CHUNK
    t = uv.shape[0]
    qrow = lambda w: pl.BlockSpec((CHUNK, w), lambda b, i: (b * nq + i, 0))
    kv = pl.BlockSpec((seq, MIX_W), lambda b, i: (b, 0))
    full = lambda shape: pl.BlockSpec(shape, lambda b, i: (0,) * len(shape))
    return pl.pallas_call(
        _gate_attn_kernel,
        grid=(batch, nq),
        in_specs=[qrow(2 * MIX_W), qrow(MIX_W), kv, kv,
                  full((2 * N_PAIR, CHUNK, CHUNK)), full((CHUNK, MIX_W)), full((1, MIX_W)),
                  full((MIX_W, MIX_W)), full((2 * CHUNK, 2 * LANES))],
        out_specs=qrow(2 * MIX_W),
        out_shape=jax.ShapeDtypeStruct((t, 2 * MIX_W), BF16),
        scratch_shapes=[pltpu.VMEM((2, seq, MIX_W), BF16), pltpu.VMEM((2, seq, MIX_W), BF16),
                        pltpu.VMEM((N_PAIR, CHUNK, 2 * LANES), F32),
                        pltpu.VMEM((N_PAIR, CHUNK, LANES), F32)],
        compiler_params=pltpu.CompilerParams(dimension_semantics=("arbitrary", "arbitrary"),
                                             vmem_limit_bytes=VMEM_LIMIT),
        name="gate_attn",
    )(uv, q, k, v, w_s, bias_full, v_gain, gmat, cmat)


def _router_epilogue(x1, gain_ref, wr_hi_ref, wr_lo_ref, ltri_ref, base_ref, is_first,
                     xn_ref, rinfo_ref, counts_ref):
    tm = x1.shape[0]
    xn = _rms_norm(x1, gain_ref[...])
    _store_row_tiles(xn_ref, xn)
    x_hi, x_lo = _split_bf16(xn)
    logits = _dot(x_hi, wr_hi_ref[...]) + _dot(x_lo, wr_hi_ref[...]) + _dot(x_hi, wr_lo_ref[...])

    lane = lax.broadcasted_iota(jnp.int32, (tm, ROUTER_LANES), 1)
    neg_inf = jnp.float32(-jnp.inf)
    lane_f = lane.astype(F32)
    big = jnp.float32(ROUTER_LANES)
    is_group = lane < N_GROUPS
    lg = jnp.where(is_group, logits, neg_inf)
    gmax = jnp.max(lg, axis=1, keepdims=True)
    gsum = jnp.sum(jnp.where(is_group, jnp.exp(logits - gmax), 0.0), axis=1, keepdims=True)
    p_top = 1.0 / gsum
    top_g = jnp.min(jnp.where(lg == gmax, lane_f, big), axis=1, keepdims=True)

    expert = lane - EXPERT_LANE0
    expert_group = (expert >> 3).astype(F32)
    in_group = (expert >= 0) & (expert < N_EXPERTS) & (expert_group == top_g)
    le = jnp.where(in_group, logits, neg_inf)
    m1 = jnp.max(le, axis=1, keepdims=True)
    i1 = jnp.min(jnp.where(le == m1, lane_f, big), axis=1, keepdims=True)
    le2 = jnp.where(lane_f == i1, neg_inf, le)
    m2 = jnp.max(le2, axis=1, keepdims=True)
    i2 = jnp.min(jnp.where(le2 == m2, lane_f, big), axis=1, keepdims=True)
    r = jnp.exp(m2 - m1)
    w1 = p_top / (1.0 + r)
    w2 = p_top * r / (1.0 + r)

    @pl.when(is_first)
    def _():
        base_ref[...] = jnp.zeros_like(base_ref)

    hit1 = lane_f == i1
    hit2 = lane_f == i2
    used = (hit1 | hit2).astype(F32)
    before = _dot(ltri_ref[...], used.astype(BF16)) + base_ref[0:1, :]
    rank1 = jnp.sum(jnp.where(hit1, before, 0.0), axis=1, keepdims=True)
    rank2 = jnp.sum(jnp.where(hit2, before, 0.0), axis=1, keepdims=True)
    new_base = base_ref[0:1, :] + jnp.sum(used, axis=0, keepdims=True)
    base_ref[...] = jnp.broadcast_to(new_base, base_ref.shape)
    counts_ref[...] = jnp.broadcast_to(new_base, counts_ref.shape)

    code1 = (i1 - EXPERT_LANE0) * RANK_SPAN + rank1
    code2 = (i2 - EXPERT_LANE0) * RANK_SPAN + rank2
    info = jnp.zeros((tm, ROUTER_LANES), F32)
    for idx, val in enumerate((code1, code2, w1, w2)):
        info = jnp.where(lane == idx, val, info)
    rinfo_ref[...] = info


def _router_specs(tm, index):
    const = lambda shape: pl.BlockSpec(shape, lambda *g: (0,) * len(shape))
    in_specs = [const((1, D_MODEL)), const((D_MODEL, ROUTER_LANES)), const((D_MODEL, ROUTER_LANES)),
                const((tm, tm))]
    out_specs = [pl.BlockSpec((tm * ROW_TILE, LANES), index), pl.BlockSpec((tm, ROUTER_LANES), index),
                 const((8, ROUTER_LANES))]
    return in_specs, out_specs


def _router_out_shapes(t):
    return [jax.ShapeDtypeStruct((t * ROW_TILE, LANES), F32), jax.ShapeDtypeStruct((t, ROUTER_LANES), F32),
            jax.ShapeDtypeStruct((8, ROUTER_LANES), F32)]


def _even_outproj_kernel(x_ref, mix_ref, wout_ref, gain_ref, wr_hi_ref, wr_lo_ref, ltri_ref,
                         x1_ref, xn_ref, rinfo_ref, counts_ref, base_ref):
    x1 = x_ref[...] + _dot(mix_ref[...], wout_ref[...])
    x1_ref[...] = x1
    _router_epilogue(x1, gain_ref, wr_hi_ref, wr_lo_ref, ltri_ref, base_ref, pl.program_id(0) == 0,
                     xn_ref, rinfo_ref, counts_ref)


def _even_outproj(x, mix, w_out, gain, wr_hi, wr_lo, ltri):
    t = x.shape[0]
    index = lambda i: (i, 0)
    r_in, r_out = _router_specs(TM_PROJ, index)
    return pl.pallas_call(
        _even_outproj_kernel,
        grid=(t // TM_PROJ,),
        in_specs=[pl.BlockSpec((TM_PROJ, D_MODEL), index), pl.BlockSpec((TM_PROJ, D_MODEL), index),
                  pl.BlockSpec((D_MODEL, D_MODEL), lambda i: (0, 0))] + r_in,
        out_specs=[pl.BlockSpec((TM_PROJ, D_MODEL), index)] + r_out,
        out_shape=[jax.ShapeDtypeStruct((t, D_MODEL), F32)] + _router_out_shapes(t),
        scratch_shapes=[pltpu.VMEM((8, ROUTER_LANES), F32)],
        compiler_params=pltpu.CompilerParams(dimension_semantics=("arbitrary",), vmem_limit_bytes=VMEM_LIMIT),
        name="even_outproj_router",
    )(x, mix, w_out, gain, wr_hi, wr_lo, ltri)


CONV_PAD = 8


def _odd_mixer_kernel(x_ref, gain_mix_ref, win_ref, convw_ref, wout_ref, gain_ref, wr_hi_ref, wr_lo_ref,
                      ltri_ref, x1_ref, xn_ref, rinfo_ref, counts_ref, base_ref, ybuf_ref):
    s = pl.program_id(1)
    tm = x_ref.shape[0]
    x = x_ref[...]
    h = _rms_norm(x, gain_mix_ref[...]).astype(BF16)
    gate_b = _dot(h, win_ref[:, 0:D_MODEL])
    gate_c = _dot(h, win_ref[:, D_MODEL:2 * D_MODEL])
    y = gate_c * _dot(h, win_ref[:, 2 * D_MODEL:3 * D_MODEL])

    @pl.when(s == 0)
    def _():
        ybuf_ref[0:CONV_PAD, :] = jnp.zeros((CONV_PAD, D_MODEL), F32)

    ybuf_ref[CONV_PAD:CONV_PAD + tm, :] = y
    y1 = ybuf_ref[CONV_PAD - 1:CONV_PAD - 1 + tm, :]
    y2 = ybuf_ref[CONV_PAD - 2:CONV_PAD - 2 + tm, :]
    conv = convw_ref[0:1, :] * y2 + convw_ref[1:2, :] * y1 + convw_ref[2:3, :] * y
    ybuf_ref[0:CONV_PAD, :] = y[tm - CONV_PAD:tm, :]
    x1 = x + _dot((gate_b * conv).astype(BF16), wout_ref[...])
    x1_ref[...] = x1
    _router_epilogue(x1, gain_ref, wr_hi_ref, wr_lo_ref, ltri_ref, base_ref,
                     (pl.program_id(0) == 0) & (s == 0), xn_ref, rinfo_ref, counts_ref)


def _odd_mixer(x, gain_mix, w_in, conv_w, w_out, gain, wr_hi, wr_lo, ltri, batch, seq):
    t = x.shape[0]
    ns = seq // TM_PROJ
    index = lambda b, s: (b * ns + s, 0)
    const = lambda shape: pl.BlockSpec(shape, lambda b, s: (0,) * len(shape))
    r_in, r_out = _router_specs(TM_PROJ, index)
    return pl.pallas_call(
        _odd_mixer_kernel,
        grid=(batch, ns),
        in_specs=[pl.BlockSpec((TM_PROJ, D_MODEL), index), const((1, D_MODEL)),
                  const((D_MODEL, 3 * D_MODEL)), const((3, D_MODEL)), const((D_MODEL, D_MODEL))] + r_in,
        out_specs=[pl.BlockSpec((TM_PROJ, D_MODEL), index)] + r_out,
        out_shape=[jax.ShapeDtypeStruct((t, D_MODEL), F32)] + _router_out_shapes(t),
        scratch_shapes=[pltpu.VMEM((8, ROUTER_LANES), F32),
                        pltpu.VMEM((CONV_PAD + TM_PROJ, D_MODEL), F32)],
        compiler_params=pltpu.CompilerParams(dimension_semantics=("arbitrary", "arbitrary"),
                                             vmem_limit_bytes=VMEM_LIMIT),
        name="odd_mixer_router",
    )(x, gain_mix, w_in, conv_w, w_out, gain, wr_hi, wr_lo, ltri)


def _row_tile(ref, first_row):
    return ref.at[pl.ds(pl.multiple_of(first_row, ROW_TILE), ROW_TILE), :]


def _for_each_row_dma(tm, copy):
    def start(blk, c):
        for u in range(DMA_UNROLL):
            for slot in range(2):
                copy(blk * DMA_UNROLL + u, slot).start(priority=slot)
        return c

    def wait(blk, c):
        for u in range(DMA_UNROLL):
            for slot in range(2):
                copy(blk * DMA_UNROLL + u, slot).wait()
        return c

    lax.fori_loop(0, tm // DMA_UNROLL, start, 0)
    lax.fori_loop(0, tm // DMA_UNROLL, wait, 0)


def _sorted_first_row(row_start_ref, code):
    return (row_start_ref[code >> 14] + (code & (RANK_SPAN - 1))) * ROW_TILE


def _dispatch_kernel(row_start_ref, last_tile_ref, n_used_ref, code_ref, xn_ref, xs_ref, zeros_ref, sem, zero_sem):
    tm = xn_ref.shape[0] // ROW_TILE
    tile_rows = TM_MOE * ROW_TILE
    n_tiles = xs_ref.shape[0] // tile_rows

    @pl.when(pl.program_id(0) == 0)
    def _():
        zeros_ref[...] = jnp.zeros_like(zeros_ref)

        def zero_copy(tile):
            start = pl.multiple_of(tile * tile_rows, tile_rows)
            return pltpu.make_async_copy(zeros_ref, xs_ref.at[pl.ds(start, tile_rows), :], zero_sem)

        def for_each_zero_tile(fn):
            def last(e, c):
                @pl.when(last_tile_ref[e] >= 0)
                def _():
                    fn(zero_copy(last_tile_ref[e]))
                return c

            def unused(tile, c):
                fn(zero_copy(tile))
                return c

            lax.fori_loop(0, N_EXPERTS, last, 0)
            lax.fori_loop(n_used_ref[0], n_tiles, unused, 0)

        for_each_zero_tile(lambda dma: dma.start())
        for_each_zero_tile(lambda dma: dma.wait())

    def copy(r, slot):
        return pltpu.make_async_copy(_row_tile(xn_ref, r * ROW_TILE),
                                     _row_tile(xs_ref, _sorted_first_row(row_start_ref, code_ref[2 * r + slot])), sem)

    _for_each_row_dma(tm, copy)


def _dispatch(row_start, last_tile, n_used, codes, xn_rows, n_tiles):
    t = xn_rows.shape[0] // ROW_TILE
    grid_spec = pltpu.PrefetchScalarGridSpec(
        num_scalar_prefetch=3,
        grid=(t // TM_DISPATCH,),
        in_specs=[pl.BlockSpec((2 * TM_DISPATCH,), lambda i, rs, lt, nu: (i,), memory_space=pltpu.SMEM),
                  pl.BlockSpec((TM_DISPATCH * ROW_TILE, LANES), lambda i, rs, lt, nu: (i, 0))],
        out_specs=pl.BlockSpec(memory_space=pl.ANY),
        scratch_shapes=[pltpu.VMEM((TM_MOE * ROW_TILE, LANES), F32),
                        pltpu.SemaphoreType.DMA(()), pltpu.SemaphoreType.DMA(())],
    )
    return pl.pallas_call(
        _dispatch_kernel,
        grid_spec=grid_spec,
        out_shape=jax.ShapeDtypeStruct((n_tiles * TM_MOE * ROW_TILE, LANES), F32),
        compiler_params=pltpu.CompilerParams(dimension_semantics=("arbitrary",)),
        name="moe_dispatch",
    )(row_start, last_tile, n_used, codes, xn_rows)


def _expert_ffn_kernel(tile_expert_ref, n_used_ref, xs_ref, wg_ref, wu_ref, wd_ref, ys_ref):
    del tile_expert_ref
    active = pl.program_id(0) < n_used_ref[0]

    @pl.when(active)
    def _():
        x = _load_row_tiles(xs_ref, TM_MOE).astype(BF16)
        g = _dot(x, wg_ref[...].astype(BF16))
        up = _dot(x, wu_ref[...].astype(BF16))
        h = (g * jax.nn.sigmoid(g)) * up
        _store_row_tiles(ys_ref, _dot(h.astype(BF16), wd_ref[...].astype(BF16)))

    @pl.when(jnp.logical_not(active))
    def _():
        ys_ref[...] = jnp.zeros_like(ys_ref)


def _expert_ffn(tile_expert, n_used, xs, w_gate, w_up, w_down, layer):
    n_tiles = xs.shape[0] // (TM_MOE * ROW_TILE)
    xs_index = lambda t, te, nu: (jnp.minimum(t, nu[0] - 1), 0)
    w_index = lambda t, te, nu: (layer, te[t], 0, 0)
    grid_spec = pltpu.PrefetchScalarGridSpec(
        num_scalar_prefetch=2,
        grid=(n_tiles,),
        in_specs=[pl.BlockSpec((TM_MOE * ROW_TILE, LANES), xs_index),
                  pl.BlockSpec((None, None, D_MODEL, EXPERT_FF), w_index),
                  pl.BlockSpec((None, None, D_MODEL, EXPERT_FF), w_index),
                  pl.BlockSpec((None, None, EXPERT_FF, D_MODEL), w_index)],
        out_specs=pl.BlockSpec((TM_MOE * ROW_TILE, LANES), lambda t, te, nu: (t, 0)),
    )
    return pl.pallas_call(
        _expert_ffn_kernel,
        grid_spec=grid_spec,
        out_shape=jax.ShapeDtypeStruct(xs.shape, F32),
        compiler_params=pltpu.CompilerParams(dimension_semantics=("arbitrary",), vmem_limit_bytes=VMEM_LIMIT),
        name="expert_ffn",
    )(tile_expert, n_used, xs, w_gate, w_up, w_down)


def _combine_kernel(row_start_ref, code_ref, x_ref, rinfo_ref, gain_ref, ys_ref, out_ref, buf0_ref, buf1_ref,
                    sem, *, final_norm):
    tm = x_ref.shape[0]
    bufs = (buf0_ref, buf1_ref)

    def copy(r, slot):
        return pltpu.make_async_copy(_row_tile(ys_ref, _sorted_first_row(row_start_ref, code_ref[2 * r + slot])),
                                     _row_tile(bufs[slot], r * ROW_TILE), sem)

    _for_each_row_dma(tm, copy)
    out = (x_ref[...] + rinfo_ref[:, 2:3] * _load_row_tiles(buf0_ref, tm)
           + rinfo_ref[:, 3:4] * _load_row_tiles(buf1_ref, tm))
    if final_norm:
        out = _rms_norm(out, gain_ref[...])
    out_ref[...] = out


def _combine(row_start, codes, x1, rinfo, gain, ys, final_norm):
    t = x1.shape[0]
    grid_spec = pltpu.PrefetchScalarGridSpec(
        num_scalar_prefetch=1,
        grid=(t // TM_COMBINE,),
        in_specs=[pl.BlockSpec((2 * TM_COMBINE,), lambda i, rs: (i,), memory_space=pltpu.SMEM),
                  pl.BlockSpec((TM_COMBINE, D_MODEL), lambda i, rs: (i, 0)),
                  pl.BlockSpec((TM_COMBINE, ROUTER_LANES), lambda i, rs: (i, 0)),
                  pl.BlockSpec((1, D_MODEL), lambda i, rs: (0, 0)),
                  pl.BlockSpec(memory_space=pl.ANY)],
        out_specs=pl.BlockSpec((TM_COMBINE, D_MODEL), lambda i, rs: (i, 0)),
        scratch_shapes=[pltpu.VMEM((TM_COMBINE * ROW_TILE, LANES), F32),
                        pltpu.VMEM((TM_COMBINE * ROW_TILE, LANES), F32),
                        pltpu.SemaphoreType.DMA(())],
    )
    return pl.pallas_call(
        functools.partial(_combine_kernel, final_norm=final_norm),
        grid_spec=grid_spec,
        out_shape=jax.ShapeDtypeStruct((t, D_MODEL), F32),
        compiler_params=pltpu.CompilerParams(dimension_semantics=("arbitrary",), vmem_limit_bytes=VMEM_LIMIT),
        name="moe_combine",
    )(row_start, codes, x1, rinfo, gain, ys)


def _moe(x1, xn, rinfo, counts, w_gate, w_up, w_down, layer, final_gain):
    t = x1.shape[0]
    n_tiles = (2 * t) // TM_MOE + N_EXPERTS
    cnt = counts[0, EXPERT_LANE0:EXPERT_LANE0 + N_EXPERTS].astype(jnp.int32)
    tiles_per_expert = (cnt + TM_MOE - 1) // TM_MOE
    tile_end = jnp.cumsum(tiles_per_expert)
    row_start = (tile_end - tiles_per_expert) * TM_MOE
    n_used = tile_end[-1:]
    codes = rinfo[:, 0:2].astype(jnp.int32).reshape(-1)
    tile_ids = jnp.minimum(jnp.arange(n_tiles, dtype=jnp.int32), n_used - 1)
    tile_expert = jnp.sum(tile_ids[:, None] >= tile_end[None, :], axis=1).astype(jnp.int32)

    last_tile = jnp.where(tiles_per_expert > 0, tile_end - 1, -1).astype(jnp.int32)
    n_used = n_used.astype(jnp.int32)
    xs = _dispatch(row_start, last_tile, n_used, codes, xn, n_tiles)
    ys = _expert_ffn(tile_expert, n_used, xs, w_gate, w_up, w_down, layer)
    gain = jnp.ones((1, D_MODEL), F32) if final_gain is None else final_gain.reshape(1, D_MODEL)
    return _combine(row_start, codes, x1, rinfo, gain, ys, final_gain is not None)


def _router_weights(router_group, router_expert):
    w = jnp.concatenate([router_group, jnp.transpose(router_expert, (1, 0, 2)).reshape(D_MODEL, N_EXPERTS)],
                        axis=1)
    w = jnp.pad(w, ((0, 0), (0, ROUTER_LANES - w.shape[1])))
    return _split_bf16(w)


def kernel(x, even_w_in, even_w_out, gmlp_w_s, gmlp_b_s, gmlp_v_gain, odd_w_in, odd_conv_w, odd_w_out,
           norm_mix, norm_ffn, router_group, router_expert, w_gate, w_up, w_down, norm_final):
    batch, seq, _ = x.shape
    depth = norm_mix.shape[0]
    xt = x.reshape(batch * seq, D_MODEL)
    ltri = (lax.broadcasted_iota(jnp.int32, (TM_PROJ, TM_PROJ), 0)
            > lax.broadcasted_iota(jnp.int32, (TM_PROJ, TM_PROJ), 1)).astype(BF16)
    lane_group = jnp.arange(MIX_W) // HEAD_DIM
    gmat = (lane_group[:, None] == lane_group[None, :]).astype(BF16)
    key = jnp.arange(CHUNK)
    cmat = jnp.concatenate([(key[:, None] >= key[None, :]).astype(BF16), jnp.ones((CHUNK, LANES), BF16)], axis=1)
    cmat = jnp.concatenate([cmat, cmat], axis=0)
    for l in range(depth):
        i = l // 2
        gain_mix = norm_mix[l].reshape(1, D_MODEL)
        gain_ffn = norm_ffn[l].reshape(1, D_MODEL)
        wr_hi, wr_lo = _router_weights(router_group[l], router_expert[l])
        if l % 2 == 0:
            uv, q, k, v = _even_inproj(xt, gain_mix, even_w_in[i].astype(BF16))
            bias_full = jnp.repeat(gmlp_b_s[i].T, HEAD_DIM, axis=1)
            mix = _gate_attn(uv, q, k, v, gmlp_w_s[i], bias_full, gmlp_v_gain[i].reshape(1, MIX_W), gmat, cmat,
                             batch, seq)
            x1, xn, rinfo, counts = _even_outproj(xt, mix, even_w_out[i].astype(BF16), gain_ffn,
                                                  wr_hi, wr_lo, ltri)
        else:
            x1, xn, rinfo, counts = _odd_mixer(xt, gain_mix, odd_w_in[i].astype(BF16), odd_conv_w[i],
                                               odd_w_out[i].astype(BF16), gain_ffn, wr_hi, wr_lo, ltri,
                                               batch, seq)
        xt = _moe(x1, xn, rinfo, counts, w_gate, w_up, w_down, l,
                  norm_final if l == depth - 1 else None)
    return xt.reshape(batch, seq, D_MODEL)
```

```python
import functools

import jax
import jax.numpy as jnp
from jax import lax
from jax.experimental import pallas as pl
from jax.experimental.pallas import tpu as pltpu

F32 = jnp.float32
BF16 = jnp.bfloat16

D_MODEL = 1024
N_PAIR = 4
HEAD_DIM = 64
MIX_W = 512
CHUNK = 128
N_GROUPS = 4
EXPERTS_PER_GROUP = 8
N_EXPERTS = 32
EXPERT_FF = 512
EPS = 1e-6
LANES = 128

ROW_TILE = D_MODEL // LANES
RANK_SPAN = 16384
DMA_UNROLL = 8
KEY_UNROLL = 2
ROUTER_LANES = 128
EXPERT_LANE0 = N_GROUPS

TM_PROJ = 512
TM_MOE = 256
TM_DISPATCH = 1024
TM_COMBINE = 1024
VMEM_LIMIT = 56 * 1024 * 1024


def _rms_norm(x, gain):
    return x * lax.rsqrt(jnp.mean(x * x, axis=-1, keepdims=True) + EPS) * gain


def _split_bf16(x):
    hi = x.astype(BF16)
    lo = (x - hi.astype(F32)).astype(BF16)
    return hi, lo


def _dot(a, b):
    return jnp.dot(a, b, preferred_element_type=F32)


def _store_row_tiles(ref, x):
    rows = x.shape[0]
    for s in range(ROW_TILE):
        ref[pl.ds(s, rows, stride=ROW_TILE), :] = x[:, s * LANES:(s + 1) * LANES]


def _load_row_tiles(ref, rows):
    return jnp.concatenate([ref[pl.ds(s, rows, stride=ROW_TILE), :] for s in range(ROW_TILE)], axis=1)


def _even_inproj_kernel(x_ref, gain_ref, w_ref, uv_ref, q_ref, k_ref, v_ref):
    h = _rms_norm(x_ref[...], gain_ref[...]).astype(BF16)
    uv = _dot(h, w_ref[:, 0:2 * MIX_W])
    uv = 0.5 * uv * (1.0 + lax.erf(uv * (0.5 ** 0.5)))
    uv_ref[...] = uv.astype(BF16)
    q_ref[...] = (_dot(h, w_ref[:, 2 * MIX_W:3 * MIX_W]) * (HEAD_DIM ** -0.5)).astype(BF16)
    k_ref[...] = _dot(h, w_ref[:, 3 * MIX_W:4 * MIX_W]).astype(BF16)
    v_ref[...] = _dot(h, w_ref[:, 4 * MIX_W:5 * MIX_W]).astype(BF16)


def _even_inproj(x, gain, w_in):
    t = x.shape[0]
    row = lambda w: pl.BlockSpec((TM_PROJ, w), lambda i: (i, 0))
    return pl.pallas_call(
        _even_inproj_kernel,
        grid=(t // TM_PROJ,),
        in_specs=[row(D_MODEL),
                  pl.BlockSpec((1, D_MODEL), lambda i: (0, 0)),
                  pl.BlockSpec((D_MODEL, 5 * MIX_W), lambda i: (0, 0))],
        out_specs=[row(2 * MIX_W), row(MIX_W), row(MIX_W), row(MIX_W)],
        out_shape=[jax.ShapeDtypeStruct((t, 2 * MIX_W), BF16)] + [jax.ShapeDtypeStruct((t, MIX_W), BF16)] * 3,
        compiler_params=pltpu.CompilerParams(dimension_semantics=("arbitrary",), vmem_limit_bytes=VMEM_LIMIT),
        name="even_inproj",
    )(x, gain, w_in)


def _gate_attn_kernel(uv_ref, q_ref, k_ref, v_ref, ws_ref, bias_ref, vgain_ref, gmat_ref, cmat2_ref, out_ref,
                      kh_ref, vh_ref, carry_ref, acc_ref):
    i = pl.program_id(1)
    row = lax.broadcasted_iota(jnp.int32, (CHUNK, CHUNK), 0)
    col = lax.broadcasted_iota(jnp.int32, (CHUNK, CHUNK), 1)
    first_half = col < HEAD_DIM

    u = uv_ref[:, 0:MIX_W].astype(F32)
    v = uv_ref[:, MIX_W:2 * MIX_W].astype(F32)
    sq_hi, sq_lo = _split_bf16(v * v)
    mean_sq = (_dot(sq_hi, gmat_ref[...]) + _dot(sq_lo, gmat_ref[...])) * (1.0 / HEAD_DIM)
    vn = (v * lax.rsqrt(mean_sq + EPS) * vgain_ref[...]).astype(BF16)
    tril = row >= col
    for p in range(N_PAIR):
        cols = slice(p * LANES, (p + 1) * LANES)
        vp = vn[:, cols]
        m0 = _dot(jnp.where(tril, ws_ref[2 * p], 0.0).astype(BF16), vp)
        m1 = _dot(jnp.where(tril, ws_ref[2 * p + 1], 0.0).astype(BF16), vp)
        mixed = jnp.where(first_half, m0, m1) + bias_ref[:, cols]
        out_ref[:, cols] = (u[:, cols] * mixed).astype(out_ref.dtype)

    row2 = lax.broadcasted_iota(jnp.int32, (CHUNK, 2 * CHUNK), 0)
    col2 = lax.broadcasted_iota(jnp.int32, (CHUNK, 2 * CHUNK), 1)
    strict2 = (col2 & (CHUNK - 1)) < row2
    pairs = range(N_PAIR)
    pcols = [slice(p * LANES, (p + 1) * LANES) for p in pairs]

    @pl.when(i == 0)
    def _():
        lane = lax.broadcasted_iota(jnp.int32, (CHUNK, MIX_W), 1)
        head0 = (lane & HEAD_DIM) == 0

        def fill(blk, c):
            rows = pl.ds(pl.multiple_of(blk * CHUNK, CHUNK), CHUNK)
            for src, dst in ((k_ref, kh_ref), (v_ref, vh_ref)):
                x = src[rows, :]
                zero = jnp.zeros_like(x)
                dst[0, rows, :] = jnp.where(head0, x, zero)
                dst[1, rows, :] = jnp.where(head0, zero, x)
            return c

        lax.fori_loop(0, k_ref.shape[0] // CHUNK, fill, 0)

    def per_head_rows(ref, rows, cols):
        return jnp.concatenate([ref[0, rows, cols], ref[1, rows, cols]], axis=0)

    def key_blocks(js, first):
        assert len(js) == 1 or not first
        rows = [pl.ds(pl.multiple_of(j * CHUNK, CHUNK), CHUNK) for j in js]
        chains = [(b, p) for b in range(len(js)) for p in pairs]
        z = {(b, p): lax.dot_general(q_ref[:, pcols[p]], per_head_rows(kh_ref, rows[b], pcols[p]),
                                     (((1,), (1,)), ((), ())), preferred_element_type=F32)
             for b, p in chains}
        suffix, total = {}, {}
        for c in chains:
            sp = jnp.maximum(z[c], 0.0) + jnp.log(1.0 + jnp.exp(-jnp.abs(z[c])))
            if first:
                sp = jnp.where(strict2, sp, 0.0)
            hi, lo = _split_bf16(sp)
            lhs = jnp.concatenate([jnp.concatenate([hi[:, 0:CHUNK], lo[:, 0:CHUNK]], axis=1),
                                   jnp.concatenate([hi[:, CHUNK:], lo[:, CHUNK:]], axis=1)], axis=0)
            s = _dot(lhs, cmat2_ref[...])
            suffix[c] = jnp.concatenate([s[0:CHUNK, 0:CHUNK], s[CHUNK:, 0:CHUNK]], axis=1)
            total[c] = jnp.concatenate([s[0:CHUNK, CHUNK:], s[CHUNK:, CHUNK:]], axis=1)
        pv = {}
        for b, p in chains:
            exponent = z[b, p] - suffix[b, p]
            if not first:
                carry = carry_ref[p]
                for earlier in range(b):
                    carry = carry + total[earlier, p]
                exponent = exponent - carry
            a = jnp.exp(exponent)
            if first:
                a = jnp.where(strict2, a, 0.0)
            pv[b, p] = _dot(a.astype(BF16), per_head_rows(vh_ref, rows[b], pcols[p]))
        for p in pairs:
            pv_sum, total_sum = pv[0, p], total[0, p]
            for b in range(1, len(js)):
                pv_sum, total_sum = pv_sum + pv[b, p], total_sum + total[b, p]
            if first:
                acc_ref[p] = pv_sum
                carry_ref[p] = total_sum
            else:
                acc_ref[p] += pv_sum
                carry_ref[p] += total_sum

    key_blocks([i], True)

    def body(step, c):
        j = i - 1 - KEY_UNROLL * step
        key_blocks([j - u for u in range(KEY_UNROLL)], False)
        return c

    lax.fori_loop(0, i // KEY_UNROLL, body, 0)

    def tail(step, c):
        key_blocks([i % KEY_UNROLL - 1 - step], False)
        return c

    lax.fori_loop(0, i % KEY_UNROLL, tail, 0)

    for p in pairs:
        out_ref[:, MIX_W + p * LANES:MIX_W + (p + 1) * LANES] = acc_ref[p].astype(out_ref.dtype)


def _gate_attn(uv, q, k, v, w_s, bias_full, v_gain, gmat, cmat, batch, seq):
    nq = seq // CHUNK
    t = uv.shape[0]
    qrow = lambda w: pl.BlockSpec((CHUNK, w), lambda b, i: (b * nq + i, 0))
    kv = pl.BlockSpec((seq, MIX_W), lambda b, i: (b, 0))
    full = lambda shape: pl.BlockSpec(shape, lambda b, i: (0,) * len(shape))
    return pl.pallas_call(
        _gate_attn_kernel,
        grid=(batch, nq),
        in_specs=[qrow(2 * MIX_W), qrow(MIX_W), kv, kv,
                  full((2 * N_PAIR, CHUNK, CHUNK)), full((CHUNK, MIX_W)), full((1, MIX_W)),
                  full((MIX_W, MIX_W)), full((2 * CHUNK, 2 * LANES))],
        out_specs=qrow(2 * MIX_W),
        out_shape=jax.ShapeDtypeStruct((t, 2 * MIX_W), BF16),
        scratch_shapes=[pltpu.VMEM((2, seq, MIX_W), BF16), pltpu.VMEM((2, seq, MIX_W), BF16),
                        pltpu.VMEM((N_PAIR, CHUNK, 2 * LANES), F32),
                        pltpu.VMEM((N_PAIR, CHUNK, LANES), F32)],
        compiler_params=pltpu.CompilerParams(dimension_semantics=("arbitrary", "arbitrary"),
                                             vmem_limit_bytes=VMEM_LIMIT),
        name="gate_attn",
    )(uv, q, k, v, w_s, bias_full, v_gain, gmat, cmat)


def _router_epilogue(x1, gain_ref, wr_hi_ref, wr_lo_ref, ltri_ref, base_ref, is_first,
                     xn_ref, rinfo_ref, counts_ref):
    tm = x1.shape[0]
    xn = _rms_norm(x1, gain_ref[...])
    _store_row_tiles(xn_ref, xn)
    x_hi, x_lo = _split_bf16(xn)
    logits = _dot(x_hi, wr_hi_ref[...]) + _dot(x_lo, wr_hi_ref[...]) + _dot(x_hi, wr_lo_ref[...])

    lane = lax.broadcasted_iota(jnp.int32, (tm, ROUTER_LANES), 1)
    neg_inf = jnp.float32(-jnp.inf)
    lane_f = lane.astype(F32)
    big = jnp.float32(ROUTER_LANES)
    is_group = lane < N_GROUPS
    lg = jnp.where(is_group, logits, neg_inf)
    gmax = jnp.max(lg, axis=1, keepdims=True)
    gsum = jnp.sum(jnp.where(is_group, jnp.exp(logits - gmax), 0.0), axis=1, keepdims=True)
    p_top = 1.0 / gsum
    top_g = jnp.min(jnp.where(lg == gmax, lane_f, big), axis=1, keepdims=True)

    expert = lane - EXPERT_LANE0
    expert_group = (expert >> 3).astype(F32)
    in_group = (expert >= 0) & (expert < N_EXPERTS) & (expert_group == top_g)
    le = jnp.where(in_group, logits, neg_inf)
    m1 = jnp.max(le, axis=1, keepdims=True)
    i1 = jnp.min(jnp.where(le == m1, lane_f, big), axis=1, keepdims=True)
    le2 = jnp.where(lane_f == i1, neg_inf, le)
    m2 = jnp.max(le2, axis=1, keepdims=True)
    i2 = jnp.min(jnp.where(le2 == m2, lane_f, big), axis=1, keepdims=True)
    r = jnp.exp(m2 - m1)
    w1 = p_top / (1.0 + r)
    w2 = p_top * r / (1.0 + r)

    @pl.when(is_first)
    def _():
        base_ref[...] = jnp.zeros_like(base_ref)

    hit1 = lane_f == i1
    hit2 = lane_f == i2
    used = (hit1 | hit2).astype(F32)
    before = _dot(ltri_ref[...], used.astype(BF16)) + base_ref[0:1, :]
    rank1 = jnp.sum(jnp.where(hit1, before, 0.0), axis=1, keepdims=True)
    rank2 = jnp.sum(jnp.where(hit2, before, 0.0), axis=1, keepdims=True)
    new_base = base_ref[0:1, :] + jnp.sum(used, axis=0, keepdims=True)
    base_ref[...] = jnp.broadcast_to(new_base, base_ref.shape)
    counts_ref[...] = jnp.broadcast_to(new_base, counts_ref.shape)

    code1 = (i1 - EXPERT_LANE0) * RANK_SPAN + rank1
    code2 = (i2 - EXPERT_LANE0) * RANK_SPAN + rank2
    info = jnp.zeros((tm, ROUTER_LANES), F32)
    for idx, val in enumerate((code1, code2, w1, w2)):
        info = jnp.where(lane == idx, val, info)
    rinfo_ref[...] = info


def _router_specs(tm, index):
    const = lambda shape: pl.BlockSpec(shape, lambda *g: (0,) * len(shape))
    in_specs = [const((1, D_MODEL)), const((D_MODEL, ROUTER_LANES)), const((D_MODEL, ROUTER_LANES)),
                const((tm, tm))]
    out_specs = [pl.BlockSpec((tm * ROW_TILE, LANES), index), pl.BlockSpec((tm, ROUTER_LANES), index),
                 const((8, ROUTER_LANES))]
    return in_specs, out_specs


def _router_out_shapes(t):
    return [jax.ShapeDtypeStruct((t * ROW_TILE, LANES), F32), jax.ShapeDtypeStruct((t, ROUTER_LANES), F32),
            jax.ShapeDtypeStruct((8, ROUTER_LANES), F32)]


def _even_outproj_kernel(x_ref, mix_ref, wout_ref, gain_ref, wr_hi_ref, wr_lo_ref, ltri_ref,
                         x1_ref, xn_ref, rinfo_ref, counts_ref, base_ref):
    x1 = x_ref[...] + _dot(mix_ref[...], wout_ref[...])
    x1_ref[...] = x1
    _router_epilogue(x1, gain_ref, wr_hi_ref, wr_lo_ref, ltri_ref, base_ref, pl.program_id(0) == 0,
                     xn_ref, rinfo_ref, counts_ref)


def _even_outproj(x, mix, w_out, gain, wr_hi, wr_lo, ltri):
    t = x.shape[0]
    index = lambda i: (i, 0)
    r_in, r_out = _router_specs(TM_PROJ, index)
    return pl.pallas_call(
        _even_outproj_kernel,
        grid=(t // TM_PROJ,),
        in_specs=[pl.BlockSpec((TM_PROJ, D_MODEL), index), pl.BlockSpec((TM_PROJ, D_MODEL), index),
                  pl.BlockSpec((D_MODEL, D_MODEL), lambda i: (0, 0))] + r_in,
        out_specs=[pl.BlockSpec((TM_PROJ, D_MODEL), index)] + r_out,
        out_shape=[jax.ShapeDtypeStruct((t, D_MODEL), F32)] + _router_out_shapes(t),
        scratch_shapes=[pltpu.VMEM((8, ROUTER_LANES), F32)],
        compiler_params=pltpu.CompilerParams(dimension_semantics=("arbitrary",), vmem_limit_bytes=VMEM_LIMIT),
        name="even_outproj_router",
    )(x, mix, w_out, gain, wr_hi, wr_lo, ltri)


CONV_PAD = 8


def _odd_mixer_kernel(x_ref, gain_mix_ref, win_ref, convw_ref, wout_ref, gain_ref, wr_hi_ref, wr_lo_ref,
                      ltri_ref, x1_ref, xn_ref, rinfo_ref, counts_ref, base_ref, ybuf_ref):
    s = pl.program_id(1)
    tm = x_ref.shape[0]
    x = x_ref[...]
    h = _rms_norm(x, gain_mix_ref[...]).astype(BF16)
    gate_b = _dot(h, win_ref[:, 0:D_MODEL])
    gate_c = _dot(h, win_ref[:, D_MODEL:2 * D_MODEL])
    y = gate_c * _dot(h, win_ref[:, 2 * D_MODEL:3 * D_MODEL])

    @pl.when(s == 0)
    def _():
        ybuf_ref[0:CONV_PAD, :] = jnp.zeros((CONV_PAD, D_MODEL), F32)

    ybuf_ref[CONV_PAD:CONV_PAD + tm, :] = y
    y1 = ybuf_ref[CONV_PAD - 1:CONV_PAD - 1 + tm, :]
    y2 = ybuf_ref[CONV_PAD - 2:CONV_PAD - 2 + tm, :]
    conv = convw_ref[0:1, :] * y2 + convw_ref[1:2, :] * y1 + convw_ref[2:3, :] * y
    ybuf_ref[0:CONV_PAD, :] = y[tm - CONV_PAD:tm, :]
    x1 = x + _dot((gate_b * conv).astype(BF16), wout_ref[...])
    x1_ref[...] = x1
    _router_epilogue(x1, gain_ref, wr_hi_ref, wr_lo_ref, ltri_ref, base_ref,
                     (pl.program_id(0) == 0) & (s == 0), xn_ref, rinfo_ref, counts_ref)


def _odd_mixer(x, gain_mix, w_in, conv_w, w_out, gain, wr_hi, wr_lo, ltri, batch, seq):
    t = x.shape[0]
    ns = seq // TM_PROJ
    index = lambda b, s: (b * ns + s, 0)
    const = lambda shape: pl.BlockSpec(shape, lambda b, s: (0,) * len(shape))
    r_in, r_out = _router_specs(TM_PROJ, index)
    return pl.pallas_call(
        _odd_mixer_kernel,
        grid=(batch, ns),
        in_specs=[pl.BlockSpec((TM_PROJ, D_MODEL), index), const((1, D_MODEL)),
                  const((D_MODEL, 3 * D_MODEL)), const((3, D_MODEL)), const((D_MODEL, D_MODEL))] + r_in,
        out_specs=[pl.BlockSpec((TM_PROJ, D_MODEL), index)] + r_out,
        out_shape=[jax.ShapeDtypeStruct((t, D_MODEL), F32)] + _router_out_shapes(t),
        scratch_shapes=[pltpu.VMEM((8, ROUTER_LANES), F32),
                        pltpu.VMEM((CONV_PAD + TM_PROJ, D_MODEL), F32)],
        compiler_params=pltpu.CompilerParams(dimension_semantics=("arbitrary", "arbitrary"),
                                             vmem_limit_bytes=VMEM_LIMIT),
        name="odd_mixer_router",
    )(x, gain_mix, w_in, conv_w, w_out, gain, wr_hi, wr_lo, ltri)


def _row_tile(ref, first_row):
    return ref.at[pl.ds(pl.multiple_of(first_row, ROW_TILE), ROW_TILE), :]


def _for_each_row_dma(tm, copy):
    def start(blk, c):
        for u in range(DMA_UNROLL):
            for slot in range(2):
                copy(blk * DMA_UNROLL + u, slot).start(priority=slot)
        return c

    def wait(blk, c):
        for u in range(DMA_UNROLL):
            for slot in range(2):
                copy(blk * DMA_UNROLL + u, slot).wait()
        return c

    lax.fori_loop(0, tm // DMA_UNROLL, start, 0)
    lax.fori_loop(0, tm // DMA_UNROLL, wait, 0)


def _sorted_first_row(row_start_ref, code):
    return (row_start_ref[code >> 14] + (code & (RANK_SPAN - 1))) * ROW_TILE


def _dispatch_kernel(row_start_ref, last_tile_ref, n_used_ref, code_ref, xn_ref, xs_ref, zeros_ref, sem, zero_sem):
    tm = xn_ref.shape[0] // ROW_TILE
    tile_rows = TM_MOE * ROW_TILE
    n_tiles = xs_ref.shape[0] // tile_rows

    @pl.when(pl.program_id(0) == 0)
    def _():
        zeros_ref[...] = jnp.zeros_like(zeros_ref)

        def zero_copy(tile):
            start = pl.multiple_of(tile * tile_rows, tile_rows)
            return pltpu.make_async_copy(zeros_ref, xs_ref.at[pl.ds(start, tile_rows), :], zero_sem)

        def for_each_zero_tile(fn):
            def last(e, c):
                @pl.when(last_tile_ref[e] >= 0)
                def _():
                    fn(zero_copy(last_tile_ref[e]))
                return c

            def unused(tile, c):
                fn(zero_copy(tile))
                return c

            lax.fori_loop(0, N_EXPERTS, last, 0)
            lax.fori_loop(n_used_ref[0], n_tiles, unused, 0)

        for_each_zero_tile(lambda dma: dma.start())
        for_each_zero_tile(lambda dma: dma.wait())

    def copy(r, slot):
        return pltpu.make_async_copy(_row_tile(xn_ref, r * ROW_TILE),
                                     _row_tile(xs_ref, _sorted_first_row(row_start_ref, code_ref[2 * r + slot])), sem)

    _for_each_row_dma(tm, copy)


def _dispatch(row_start, last_tile, n_used, codes, xn_rows, n_tiles):
    t = xn_rows.shape[0] // ROW_TILE
    grid_spec = pltpu.PrefetchScalarGridSpec(
        num_scalar_prefetch=3,
        grid=(t // TM_DISPATCH,),
        in_specs=[pl.BlockSpec((2 * TM_DISPATCH,), lambda i, rs, lt, nu: (i,), memory_space=pltpu.SMEM),
                  pl.BlockSpec((TM_DISPATCH * ROW_TILE, LANES), lambda i, rs, lt, nu: (i, 0))],
        out_specs=pl.BlockSpec(memory_space=pl.ANY),
        scratch_shapes=[pltpu.VMEM((TM_MOE * ROW_TILE, LANES), F32),
                        pltpu.SemaphoreType.DMA(()), pltpu.SemaphoreType.DMA(())],
    )
    return pl.pallas_call(
        _dispatch_kernel,
        grid_spec=grid_spec,
        out_shape=jax.ShapeDtypeStruct((n_tiles * TM_MOE * ROW_TILE, LANES), F32),
        compiler_params=pltpu.CompilerParams(dimension_semantics=("arbitrary",)),
        name="moe_dispatch",
    )(row_start, last_tile, n_used, codes, xn_rows)


def _expert_ffn_kernel(tile_expert_ref, n_used_ref, xs_ref, wg_ref, wu_ref, wd_ref, ys_ref):
    del tile_expert_ref
    active = pl.program_id(0) < n_used_ref[0]

    @pl.when(active)
    def _():
        x = _load_row_tiles(xs_ref, TM_MOE).astype(BF16)
        g = _dot(x, wg_ref[...].astype(BF16))
        up = _dot(x, wu_ref[...].astype(BF16))
        h = (g * jax.nn.sigmoid(g)) * up
        _store_row_tiles(ys_ref, _dot(h.astype(BF16), wd_ref[...].astype(BF16)))

    @pl.when(jnp.logical_not(active))
    def _():
        ys_ref[...] = jnp.zeros_like(ys_ref)


def _expert_ffn(tile_expert, n_used, xs, w_gate, w_up, w_down, layer):
    n_tiles = xs.shape[0] // (TM_MOE * ROW_TILE)
    xs_index = lambda t, te, nu: (jnp.minimum(t, nu[0] - 1), 0)
    w_index = lambda t, te, nu: (layer, te[t], 0, 0)
    grid_spec = pltpu.PrefetchScalarGridSpec(
        num_scalar_prefetch=2,
        grid=(n_tiles,),
        in_specs=[pl.BlockSpec((TM_MOE * ROW_TILE, LANES), xs_index),
                  pl.BlockSpec((None, None, D_MODEL, EXPERT_FF), w_index),
                  pl.BlockSpec((None, None, D_MODEL, EXPERT_FF), w_index),
                  pl.BlockSpec((None, None, EXPERT_FF, D_MODEL), w_index)],
        out_specs=pl.BlockSpec((TM_MOE * ROW_TILE, LANES), lambda t, te, nu: (t, 0)),
    )
    return pl.pallas_call(
        _expert_ffn_kernel,
        grid_spec=grid_spec,
        out_shape=jax.ShapeDtypeStruct(xs.shape, F32),
        compiler_params=pltpu.CompilerParams(dimension_semantics=("arbitrary",), vmem_limit_bytes=VMEM_LIMIT),
        name="expert_ffn",
    )(tile_expert, n_used, xs, w_gate, w_up, w_down)


def _combine_kernel(row_start_ref, code_ref, x_ref, rinfo_ref, gain_ref, ys_ref, out_ref, buf0_ref, buf1_ref,
                    sem, *, final_norm):
    tm = x_ref.shape[0]
    bufs = (buf0_ref, buf1_ref)

    def copy(r, slot):
        return pltpu.make_async_copy(_row_tile(ys_ref, _sorted_first_row(row_start_ref, code_ref[2 * r + slot])),
                                     _row_tile(bufs[slot], r * ROW_TILE), sem)

    _for_each_row_dma(tm, copy)
    out = (x_ref[...] + rinfo_ref[:, 2:3] * _load_row_tiles(buf0_ref, tm)
           + rinfo_ref[:, 3:4] * _load_row_tiles(buf1_ref, tm))
    if final_norm:
        out = _rms_norm(out, gain_ref[...])
    out_ref[...] = out


def _combine(row_start, codes, x1, rinfo, gain, ys, final_norm):
    t = x1.shape[0]
    grid_spec = pltpu.PrefetchScalarGridSpec(
        num_scalar_prefetch=1,
        grid=(t // TM_COMBINE,),
        in_specs=[pl.BlockSpec((2 * TM_COMBINE,), lambda i, rs: (i,), memory_space=pltpu.SMEM),
                  pl.BlockSpec((TM_COMBINE, D_MODEL), lambda i, rs: (i, 0)),
                  pl.BlockSpec((TM_COMBINE, ROUTER_LANES), lambda i, rs: (i, 0)),
                  pl.BlockSpec((1, D_MODEL), lambda i, rs: (0, 0)),
                  pl.BlockSpec(memory_space=pl.ANY)],
        out_specs=pl.BlockSpec((TM_COMBINE, D_MODEL), lambda i, rs: (i, 0)),
        scratch_shapes=[pltpu.VMEM((TM_COMBINE * ROW_TILE, LANES), F32),
                        pltpu.VMEM((TM_COMBINE * ROW_TILE, LANES), F32),
                        pltpu.SemaphoreType.DMA(())],
    )
    return pl.pallas_call(
        functools.partial(_combine_kernel, final_norm=final_norm),
        grid_spec=grid_spec,
        out_shape=jax.ShapeDtypeStruct((t, D_MODEL), F32),
        compiler_params=pltpu.CompilerParams(dimension_semantics=("arbitrary",), vmem_limit_bytes=VMEM_LIMIT),
        name="moe_combine",
    )(row_start, codes, x1, rinfo, gain, ys)


def _moe(x1, xn, rinfo, counts, w_gate, w_up, w_down, layer, final_gain):
    t = x1.shape[0]
    n_tiles = (2 * t) // TM_MOE + N_EXPERTS
    cnt = counts[0, EXPERT_LANE0:EXPERT_LANE0 + N_EXPERTS].astype(jnp.int32)
    tiles_per_expert = (cnt + TM_MOE - 1) // TM_MOE
    tile_end = jnp.cumsum(tiles_per_expert)
    row_start = (tile_end - tiles_per_expert) * TM_MOE
    n_used = tile_end[-1:]
    codes = rinfo[:, 0:2].astype(jnp.int32).reshape(-1)
    tile_ids = jnp.minimum(jnp.arange(n_tiles, dtype=jnp.int32), n_used - 1)
    tile_expert = jnp.sum(tile_ids[:, None] >= tile_end[None, :], axis=1).astype(jnp.int32)

    last_tile = jnp.where(tiles_per_expert > 0, tile_end - 1, -1).astype(jnp.int32)
    n_used = n_used.astype(jnp.int32)
    xs = _dispatch(row_start, last_tile, n_used, codes, xn, n_tiles)
    ys = _expert_ffn(tile_expert, n_used, xs, w_gate, w_up, w_down, layer)
    gain = jnp.ones((1, D_MODEL), F32) if final_gain is None else final_gain.reshape(1, D_MODEL)
    return _combine(row_start, codes, x1, rinfo, gain, ys, final_gain is not None)


def _router_weights(router_group, router_expert):
    w = jnp.concatenate([router_group, jnp.transpose(router_expert, (1, 0, 2)).reshape(D_MODEL, N_EXPERTS)],
                        axis=1)
    w = jnp.pad(w, ((0, 0), (0, ROUTER_LANES - w.shape[1])))
    return _split_bf16(w)


def kernel(x, even_w_in, even_w_out, gmlp_w_s, gmlp_b_s, gmlp_v_gain, odd_w_in, odd_conv_w, odd_w_out,
           norm_mix, norm_ffn, router_group, router_expert, w_gate, w_up, w_down, norm_final):
    batch, seq, _ = x.shape
    depth = norm_mix.shape[0]
    xt = x.reshape(batch * seq, D_MODEL)
    ltri = (lax.broadcasted_iota(jnp.int32, (TM_PROJ, TM_PROJ), 0)
            > lax.broadcasted_iota(jnp.int32, (TM_PROJ, TM_PROJ), 1)).astype(BF16)
    lane_group = jnp.arange(MIX_W) // HEAD_DIM
    gmat = (lane_group[:, None] == lane_group[None, :]).astype(BF16)
    key = jnp.arange(CHUNK)
    cmat = jnp.concatenate([(key[:, None] >= key[None, :]).astype(BF16), jnp.ones((CHUNK, LANES), BF16)], axis=1)
    cmat = jnp.concatenate([cmat, cmat], axis=0)
    for l in range(depth):
        i = l // 2
        gain_mix = norm_mix[l].reshape(1, D_MODEL)
        gain_ffn = norm_ffn[l].reshape(1, D_MODEL)
        wr_hi, wr_lo = _router_weights(router_group[l], router_expert[l])
        if l % 2 == 0:
            uv, q, k, v = _even_inproj(xt, gain_mix, even_w_in[i].astype(BF16))
            bias_full = jnp.repeat(gmlp_b_s[i].T, HEAD_DIM, axis=1)
            mix = _gate_attn(uv, q, k, v, gmlp_w_s[i], bias_full, gmlp_v_gain[i].reshape(1, MIX_W), gmat, cmat,
                             batch, seq)
            x1, xn, rinfo, counts = _even_outproj(xt, mix, even_w_out[i].astype(BF16), gain_ffn,
                                                  wr_hi, wr_lo, ltri)
        else:
            x1, xn, rinfo, counts = _odd_mixer(xt, gain_mix, odd_w_in[i].astype(BF16), odd_conv_w[i],
                                               odd_w_out[i].astype(BF16), gain_ffn, wr_hi, wr_lo, ltri,
                                               batch, seq)
        xt = _moe(x1, xn, rinfo, counts, w_gate, w_up, w_down, l,
                  norm_final if l == depth - 1 else None)
    return xt.reshape(batch, seq, D_MODEL)
```
